```python
import jax, jax.numpy as jnp
from jax import lax
import numpy as np

D_MODEL = 1024
BATCH = 8
SEQ = 2048
DEPTH = 2

MIX_WIDTH = D_MODEL
RWKV_WIDTH = MIX_WIDTH // 2
RWKV_HEAD = 64
RWKV_HEADS = RWKV_WIDTH // RWKV_HEAD
W_LORA = 64
A_LORA = 64
G_LORA = 128
LRU_WIDTH = MIX_WIDTH - RWKV_WIDTH
LRU_BLOCKS = 8
LRU_BLOCK = LRU_WIDTH // LRU_BLOCKS
LRU_CONV = 4
LRU_C = 8.0
S5_WIDTH = MIX_WIDTH
S5_GROUP = 16
S5_GROUPS = S5_WIDTH // S5_GROUP
S5_STATE = 64
D_FF = 2816
FFN_CONV = 3
NORM_EPS = 1e-6
GN_EPS = 64e-5
N_EVEN = (DEPTH + 1) // 2
N_ODD = DEPTH // 2
RWKV_SHIFT_COLS = 3 * RWKV_WIDTH + W_LORA + A_LORA + G_LORA
EVEN_IN_COLS = RWKV_SHIFT_COLS + 2 * LRU_WIDTH

kernel_name = "hybrid_rwkv7_rglru_s5_convffn"


def rms_norm(x, g):
    x32 = x.astype(jnp.float32)
    y = x32 * lax.rsqrt(jnp.mean(x32 * x32, axis=-1, keepdims=True) + NORM_EPS)
    return (y * g.astype(jnp.float32)).astype(x.dtype)


def causal_dwconv(x, w, b):
    k, c = w.shape
    y = lax.conv_general_dilated(x, w[:, None, :].astype(x.dtype), window_strides=(1,),
                                 padding=[(k - 1, 0)], dimension_numbers=('NWC', 'WIO', 'NWC'),
                                 feature_group_count=c)
    return y + b.astype(y.dtype)


def token_shift(p):
    return jnp.pad(p, ((0, 0), (1, 0), (0, 0)))[:, :-1]


def _lin_op(e1, e2):
    a1, b1 = e1
    a2, b2 = e2
    return a1 * a2, a2 * b1 + b2


def _cplx_op(e1, e2):
    a1r, a1i, b1r, b1i = e1
    a2r, a2i, b2r, b2i = e2
    return (a1r * a2r - a1i * a2i, a1r * a2i + a1i * a2r,
            a2r * b1r - a2i * b1i + b2r, a2r * b1i + a2i * b1r + b2i)


def wkv7(r, w, k, v, z, b):
    bn, s, h, n = r.shape

    def step(state, inp):
        r_t, w_t, k_t, v_t, z_t, b_t = inp
        sa = jnp.einsum('bhvk,bhk->bhv', state, z_t)
        state = (state * w_t[:, :, None, :] + sa[..., None] * b_t[:, :, None, :]
                 + v_t[..., None] * k_t[:, :, None, :])
        return state, jnp.einsum('bhvk,bhk->bhv', state, r_t)

    xs = tuple(jnp.swapaxes(t.astype(jnp.float32), 0, 1) for t in (r, w, k, v, z, b))
    state0 = jnp.zeros((bn, h, n, n), jnp.float32)
    _, ys = lax.scan(step, state0, xs)
    return jnp.swapaxes(ys, 0, 1)


def even_mixer(xn, w_in, mu, w0, w2, a0, a2, g2, k_k, k_a, r_k, ln_w, ln_b,
               conv_w, conv_b, gate_a_w, gate_a_b, gate_x_w, gate_x_b, lru_lambda, w_out):
    bn, s, _ = xn.shape
    f32 = jnp.float32
    rw = RWKV_WIDTH
    p = (xn @ w_in).astype(f32)
    pa = p[..., :RWKV_SHIFT_COLS]
    pa = pa + mu.astype(f32) * (token_shift(pa) - pa)
    r, k, v, wd, ad, gd = jnp.split(pa, [rw, 2 * rw, 3 * rw, 3 * rw + W_LORA, 3 * rw + W_LORA + A_LORA], axis=-1)
    w_log = -jax.nn.softplus(-(w0.astype(f32) + jnp.tanh(wd) @ w2.astype(f32))) - 0.5
    decay = jnp.exp(-jnp.exp(w_log))
    a = jax.nn.sigmoid(a0.astype(f32) + ad @ a2.astype(f32))
    g = jax.nn.sigmoid(gd) @ g2.astype(f32)
    hs = lambda t: t.reshape(bn, s, RWKV_HEADS, RWKV_HEAD)
    kk = hs(k * k_k.astype(f32))
    kk = kk / jnp.maximum(jnp.sqrt(jnp.sum(kk * kk, axis=-1, keepdims=True)), 1e-12)
    k = k * (1.0 + (a - 1.0) * k_a.astype(f32))
    y = wkv7(hs(r), hs(decay), hs(k), hs(v), -kk, kk * hs(a))
    mean = jnp.mean(y, axis=-1, keepdims=True)
    var = jnp.mean(jnp.square(y - mean), axis=-1, keepdims=True)
    y = ((y - mean) * lax.rsqrt(var + GN_EPS)).reshape(bn, s, rw) * ln_w.astype(f32) + ln_b.astype(f32)
    bonus = jnp.sum(hs(r) * hs(k) * r_k.astype(f32), axis=-1, keepdims=True) * hs(v)
    y_a = (y + bonus.reshape(bn, s, rw)) * g
    bx = p[..., RWKV_SHIFT_COLS:RWKV_SHIFT_COLS + LRU_WIDTH]
    bg = p[..., RWKV_SHIFT_COLS + LRU_WIDTH:]
    xc = causal_dwconv(bx, conv_w.astype(f32), conv_b.astype(f32))
    xb = xc.reshape(bn, s, LRU_BLOCKS, LRU_BLOCK)
    gr = jax.nn.sigmoid(jnp.einsum('bshi,hij->bshj', xb, gate_a_w.astype(f32)).reshape(bn, s, LRU_WIDTH) + gate_a_b.astype(f32))
    gi = jax.nn.sigmoid(jnp.einsum('bshi,hij->bshj', xb, gate_x_w.astype(f32)).reshape(bn, s, LRU_WIDTH) + gate_x_b.astype(f32))
    log_a = -LRU_C * gr * jax.nn.softplus(-lru_lambda.astype(f32))
    a_t = jnp.exp(log_a)
    u = xc * gi * jnp.sqrt(-jnp.expm1(2.0 * log_a))
    _, h = lax.associative_scan(_lin_op, (a_t, u), axis=1)
    y_b = h * jax.nn.gelu(bg)
    y_cat = jnp.concatenate([y_a, y_b], axis=-1).astype(xn.dtype)
    return y_cat @ w_out


def odd_mixer(xn, w_in, A_re, A_im, log_dt, B_re, B_im, C_re, C_im, D, w_glu):
    bn, s, _ = xn.shape
    f32 = jnp.float32
    u = (xn @ w_in).astype(f32)
    ug = u.reshape(bn, s, S5_GROUPS, S5_GROUP)
    lam_re = jnp.minimum(A_re.astype(f32), -1e-4)
    lam_im = A_im.astype(f32)
    dt = jnp.exp(log_dt.astype(f32))[:, None]
    mag = jnp.exp(lam_re * dt)
    ab_re = mag * jnp.cos(lam_im * dt)
    ab_im = mag * jnp.sin(lam_im * dt)
    den = lam_re * lam_re + lam_im * lam_im
    zr = ab_re - 1.0
    q_re = (zr * lam_re + ab_im * lam_im) / den
    q_im = (ab_im * lam_re - zr * lam_im) / den
    b_re = B_re.astype(f32)
    b_im = B_im.astype(f32)
    bb_re = q_re[..., None] * b_re - q_im[..., None] * b_im
    bb_im = q_re[..., None] * b_im + q_im[..., None] * b_re
    bu_re = jnp.einsum('bsgc,gnc->bsgn', ug, bb_re)
    bu_im = jnp.einsum('bsgc,gnc->bsgn', ug, bb_im)
    a_re = jnp.broadcast_to(ab_re[None, None], (1, s, S5_GROUPS, S5_STATE))
    a_im = jnp.broadcast_to(ab_im[None, None], (1, s, S5_GROUPS, S5_STATE))
    _, _, st_re, st_im = lax.associative_scan(_cplx_op, (a_re, a_im, bu_re, bu_im), axis=1)
    y = (jnp.einsum('gcn,bsgn->bsgc', C_re.astype(f32), st_re)
         - jnp.einsum('gcn,bsgn->bsgc', C_im.astype(f32), st_im)).reshape(bn, s, S5_WIDTH)
    y = jax.nn.gelu(y + D.astype(f32) * u).astype(xn.dtype)
    val, gate = jnp.split(y @ w_glu, 2, axis=-1)
    return val * jax.nn.sigmoid(gate)


def conv_ffn(xn, w_up, conv_w, conv_b, w_down):
    h = causal_dwconv(xn @ w_up, conv_w, conv_b)
    gate, val = jnp.split(h, 2, axis=-1)
    return (jax.nn.silu(gate) * val) @ w_down


def setup_inputs(seed: int = 0) -> dict:
    key = jax.random.key(seed)
    ks = iter(jax.random.split(key, 64))
    f32 = jnp.float32
    nrm = lambda shape, scale: jax.random.normal(next(ks), shape, f32) * scale
    uni = lambda shape, lo, hi: jax.random.uniform(next(ks), shape, f32, lo, hi)
    rw = RWKV_WIDTH
    inp = {}
    inp['x'] = nrm((BATCH, SEQ, D_MODEL), 1.0)
    inp['e_norm_g'] = 1.0 + nrm((N_EVEN, D_MODEL), 0.02)
    inp['e_w_in'] = nrm((N_EVEN, D_MODEL, EVEN_IN_COLS), D_MODEL ** -0.5)
    inp['e_mu'] = uni((N_EVEN, RWKV_SHIFT_COLS), 0.0, 1.0)
    inp['e_w0'] = uni((N_EVEN, rw), -6.0, -1.0)
    inp['e_w2'] = nrm((N_EVEN, W_LORA, rw), 0.1 * W_LORA ** -0.5)
    inp['e_a0'] = nrm((N_EVEN, rw), 0.1)
    inp['e_a2'] = nrm((N_EVEN, A_LORA, rw), 0.1 * A_LORA ** -0.5)
    inp['e_g2'] = nrm((N_EVEN, G_LORA, rw), G_LORA ** -0.5)
    inp['e_k_k'] = 0.85 + nrm((N_EVEN, rw), 0.02)
    inp['e_k_a'] = 1.0 + nrm((N_EVEN, rw), 0.02)
    inp['e_r_k'] = nrm((N_EVEN, RWKV_HEADS, RWKV_HEAD), 0.1)
    inp['e_ln_w'] = 1.0 + nrm((N_EVEN, rw), 0.02)
    inp['e_ln_b'] = nrm((N_EVEN, rw), 0.02)
    inp['e_conv_w'] = nrm((N_EVEN, LRU_CONV, LRU_WIDTH), LRU_CONV ** -0.5)
    inp['e_conv_b'] = nrm((N_EVEN, LRU_WIDTH), 0.02)
    inp['e_gate_a_w'] = nrm((N_EVEN, LRU_BLOCKS, LRU_BLOCK, LRU_BLOCK), LRU_BLOCK ** -0.5)
    inp['e_gate_a_b'] = nrm((N_EVEN, LRU_WIDTH), 0.02)
    inp['e_gate_x_w'] = nrm((N_EVEN, LRU_BLOCKS, LRU_BLOCK, LRU_BLOCK), LRU_BLOCK ** -0.5)
    inp['e_gate_x_b'] = nrm((N_EVEN, LRU_WIDTH), 0.02)
    a_c = uni((N_EVEN, LRU_WIDTH), 0.9, 0.999)
    a_base = a_c ** (1.0 / LRU_C)
    inp['e_lru_lambda'] = jnp.log(a_base) - jnp.log1p(-a_base)
    inp['e_w_out'] = nrm((N_EVEN, MIX_WIDTH, D_MODEL), MIX_WIDTH ** -0.5)
    n_idx = jnp.arange(S5_STATE, dtype=f32)
    inp['o_norm_g'] = 1.0 + nrm((N_ODD, D_MODEL), 0.02)
    inp['o_w_in'] = nrm((N_ODD, D_MODEL, S5_WIDTH), D_MODEL ** -0.5)
    inp['o_A_re'] = -0.5 + nrm((N_ODD, S5_GROUPS, S5_STATE), 0.01)
    inp['o_A_im'] = jnp.pi * n_idx + nrm((N_ODD, S5_GROUPS, S5_STATE), 0.01)
    inp['o_log_dt'] = uni((N_ODD, S5_GROUPS), float(np.log(1e-3)), float(np.log(1e-1)))
    inp['o_B_re'] = nrm((N_ODD, S5_GROUPS, S5_STATE, S5_GROUP), (2 * S5_GROUP) ** -0.5)
    inp['o_B_im'] = nrm((N_ODD, S5_GROUPS, S5_STATE, S5_GROUP), (2 * S5_GROUP) ** -0.5)
    inp['o_C_re'] = nrm((N_ODD, S5_GROUPS, S5_GROUP, S5_STATE), S5_STATE ** -0.5)
    inp['o_C_im'] = nrm((N_ODD, S5_GROUPS, S5_GROUP, S5_STATE), S5_STATE ** -0.5)
    inp['o_D'] = nrm((N_ODD, S5_WIDTH), 1.0)
    inp['o_w_glu'] = nrm((N_ODD, S5_WIDTH, 2 * D_MODEL), S5_WIDTH ** -0.5)
    inp['f_norm_g'] = 1.0 + nrm((DEPTH, D_MODEL), 0.02)
    inp['f_w_up'] = nrm((DEPTH, D_MODEL, 2 * D_FF), D_MODEL ** -0.5)
    inp['f_conv_w'] = nrm((DEPTH, FFN_CONV, 2 * D_FF), FFN_CONV ** -0.5)
    inp['f_conv_b'] = nrm((DEPTH, 2 * D_FF), 0.02)
    inp['f_w_down'] = nrm((DEPTH, D_FF, D_MODEL), D_FF ** -0.5)
    inp['final_norm_g'] = 1.0 + nrm((D_MODEL,), 0.02)
    return inp


def reference(x, e_norm_g, e_w_in, e_mu, e_w0, e_w2, e_a0, e_a2, e_g2, e_k_k, e_k_a, e_r_k,
              e_ln_w, e_ln_b, e_conv_w, e_conv_b, e_gate_a_w, e_gate_a_b, e_gate_x_w, e_gate_x_b,
              e_lru_lambda, e_w_out, o_norm_g, o_w_in, o_A_re, o_A_im, o_log_dt, o_B_re, o_B_im,
              o_C_re, o_C_im, o_D, o_w_glu, f_norm_g, f_w_up, f_conv_w, f_conv_b, f_w_down,
              final_norm_g):
    for i in range(DEPTH):
        j = i // 2
        if i % 2 == 0:
            x = x + even_mixer(rms_norm(x, e_norm_g[j]), e_w_in[j], e_mu[j], e_w0[j], e_w2[j],
                               e_a0[j], e_a2[j], e_g2[j], e_k_k[j], e_k_a[j], e_r_k[j],
                               e_ln_w[j], e_ln_b[j], e_conv_w[j], e_conv_b[j],
                               e_gate_a_w[j], e_gate_a_b[j], e_gate_x_w[j], e_gate_x_b[j],
                               e_lru_lambda[j], e_w_out[j])
        else:
            x = x + odd_mixer(rms_norm(x, o_norm_g[j]), o_w_in[j], o_A_re[j], o_A_im[j],
                              o_log_dt[j], o_B_re[j], o_B_im[j], o_C_re[j], o_C_im[j],
                              o_D[j], o_w_glu[j])
        x = x + conv_ffn(rms_norm(x, f_norm_g[i]), f_w_up[i], f_conv_w[i], f_conv_b[i], f_w_down[i])
    return rms_norm(x, final_norm_g)
```

```python
import functools

import jax
import jax.numpy as jnp
from jax import lax
from jax.experimental import pallas as pl
from jax.experimental.pallas import tpu as pltpu

F32 = jnp.float32
BF16 = jnp.bfloat16

D_MODEL = 1024
RWKV_WIDTH = 512
RWKV_HEAD = 64
RWKV_HEADS = 8
W_LORA = 64
A_LORA = 64
G_LORA = 128
LRU_WIDTH = 512
LRU_BLOCKS = 8
LRU_BLOCK = 64
LRU_CONV = 4
LRU_C = 8.0
S5_WIDTH = 1024
S5_GROUP = 16
S5_GROUPS = 64
S5_STATE = 64
D_FF = 2816
FFN_CONV = 3
NORM_EPS = 1e-6
GN_EPS = 64e-5
RWKV_SHIFT_COLS = 3 * RWKV_WIDTH + W_LORA + A_LORA + G_LORA
EVEN_IN_COLS = RWKV_SHIFT_COLS + 2 * LRU_WIDTH

SUBLANES = 8
HALO = 16
TM_MIX = 256
WKV_CHUNK = 64
TM_FFN = 512
TF_FFN = 1408
S5_LANES = 256
VMEM_LIMIT = 56 * 1024 * 1024


def _dot(a, b):
    return jnp.dot(a.astype(BF16), b.astype(BF16), preferred_element_type=F32)


def _dot_nt(a, b):
    return lax.dot_general(a.astype(BF16), b.astype(BF16), (((1,), (1,)), ((), ())),
                           preferred_element_type=F32)


def _dot_tn(a, b):
    return lax.dot_general(a.astype(BF16), b.astype(BF16), (((0,), (0,)), ((), ())),
                           preferred_element_type=F32)


def _split(x, n):
    parts = []
    for _ in range(n - 1):
        h = x.astype(BF16)
        parts.append(h)
        x = x - h.astype(F32)
    parts.append(x.astype(BF16))
    return parts


def _dot_split_rhs(a_bf16, b, n):
    acc = None
    for piece in _split(b, n):
        t = jnp.dot(a_bf16, piece, preferred_element_type=F32)
        acc = t if acc is None else acc + t
    return acc


def _dot_split_lhs(a, b_bf16, n):
    acc = None
    for piece in _split(a, n):
        t = jnp.dot(piece, b_bf16, preferred_element_type=F32)
        acc = t if acc is None else acc + t
    return acc


def _dot_hi(a, b):
    ah, al = _split(a, 2)
    bh, bl = _split(b, 2)
    return (jnp.dot(ah, bh, preferred_element_type=F32)
            + jnp.dot(al, bh, preferred_element_type=F32)
            + jnp.dot(ah, bl, preferred_element_type=F32))


def _rms(x, g):
    ms = jnp.mean(x * x, axis=-1, keepdims=True)
    return x * lax.rsqrt(ms + NORM_EPS) * g


def _shift_rows(x, d, fill):
    rows = lax.broadcasted_iota(jnp.int32, x.shape, 0)
    return jnp.where(rows >= d, pltpu.roll(x, d, 0), fill)


def _even_body(x_ref, halo_ref, ng_ref, win_ref, mu_ref, w0_ref, wwa_ref, a0_ref, g2_ref,
               kk_ref, ka_ref, rk_ref, lnw_ref, lnb_ref, cw_ref, cb_ref, gaw_ref, gab_ref,
               gxw_ref, gxb_ref, lam_ref, hones_ref, wout_ref, o_ref,
               p_s, state_s, hcar_s, *, tm):
    i = pl.program_id(1)
    rw = RWKV_WIDTH
    hd = RWKV_HEAD
    nchunk = tm // WKV_CHUNK

    @pl.when(i == 0)
    def _():
        state_s[...] = jnp.zeros_like(state_s)
        hcar_s[...] = jnp.zeros_like(hcar_s)

    x = x_ref[...]
    halo = jnp.where(i == 0, 0.0, halo_ref[...])
    ng = ng_ref[...]
    xn = jnp.concatenate([_rms(halo, ng), _rms(x, ng)], axis=0).astype(BF16)
    p_s[...] = jnp.dot(xn, win_ref[...], preferred_element_type=F32)

    pa = p_s[pl.ds(HALO, tm), 0:RWKV_SHIFT_COLS]
    pp = p_s[pl.ds(HALO - 1, tm), 0:RWKV_SHIFT_COLS]
    pm = pa + mu_ref[...] * (pp - pa)
    r = pm[:, 0:rw]
    k = pm[:, rw:2 * rw]
    v = pm[:, 2 * rw:3 * rw]
    wa = pm[:, 3 * rw:3 * rw + W_LORA + A_LORA]
    gd = pm[:, 3 * rw + W_LORA + A_LORA:RWKV_SHIFT_COLS]
    lane = lax.broadcasted_iota(jnp.int32, wa.shape, 1)
    wa = jnp.where(lane < W_LORA, jnp.tanh(wa), wa)
    lora = _dot(wa, wwa_ref[...])
    w_log = -jax.nn.softplus(-(w0_ref[...] + lora[:, 0:rw])) - 0.5
    logw = -jnp.exp(w_log)
    a = jax.nn.sigmoid(a0_ref[...] + lora[:, rw:2 * rw])
    g = _dot(jax.nn.sigmoid(gd), g2_ref[...])
    hones = hones_ref[...]
    kk = k * kk_ref[...]
    kk = kk / jnp.maximum(jnp.sqrt(_dot_split_lhs(kk * kk, hones, 2)), 1e-12)
    k = k * (1.0 + (a - 1.0) * ka_ref[...])
    bonus = _dot_split_lhs(r * k * rk_ref[...], hones, 2) * v
    zv = -kk
    bv = kk * a

    row = lax.broadcasted_iota(jnp.int32, (tm, tm), 0)
    col = lax.broadcasted_iota(jnp.int32, (tm, tm), 1)
    same = (row // WKV_CHUNK) == (col // WKV_CHUNK)
    incl = same & (row >= col)
    strict = same & (row > col)
    cum = _dot_split_rhs(incl.astype(BF16), logw, 3)
    tot = _dot_split_rhs(same.astype(BF16), logw, 3)
    inv_p = jnp.exp(-cum)
    to_end = jnp.exp(tot - cum)
    p_tot = jnp.exp(tot)
    rt = r * jnp.exp(cum)
    zt = zv * jnp.exp(cum - logw)
    kt = k * inv_p
    bt = bv * inv_p
    kh = k * to_end
    bh = bv * to_end
    eye = (row == col).astype(F32)

    ys = []
    for h in range(RWKV_HEADS):
        hs = slice(h * hd, (h + 1) * hd)
        z_h, r_h, k_h, b_h, v_h = zt[:, hs], rt[:, hs], kt[:, hs], bt[:, hs], v[:, hs]
        kh_h, bh_h, pt_h = kh[:, hs], bh[:, hs], p_tot[:, hs]
        zr = jnp.concatenate([z_h, r_h], axis=0)
        bk = jnp.concatenate([b_h, k_h], axis=0)
        aa = _dot_nt(zr, bk)
        n_zb = jnp.where(strict, aa[0:tm, 0:tm], 0.0)
        a_zk = jnp.where(strict, aa[0:tm, tm:2 * tm], 0.0)
        a_rb = jnp.where(incl, aa[tm:2 * tm, 0:tm], 0.0)
        a_rk = jnp.where(incl, aa[tm:2 * tm, tm:2 * tm], 0.0)
        xp = n_zb
        t_inv = eye + n_zb
        span = 2
        while span < WKV_CHUNK:
            xp = _dot_hi(xp, xp)
            t_inv = t_inv + _dot_hi(t_inv, xp)
            span *= 2
        wv = _dot(a_zk, v_h)
        tzu = _dot(t_inv, jnp.concatenate([z_h, wv], axis=1))
        ar = _dot(a_rb, tzu)
        m_y = r_h + ar[:, 0:hd]
        y1 = ar[:, hd:2 * hd] + _dot(a_rk, v_h)
        tz, u0 = tzu[:, 0:hd], tzu[:, hd:2 * hd]
        st = state_s[h]
        y_rows = []
        for c in range(nchunk):
            cs = slice(c * WKV_CHUNK, (c + 1) * WKV_CHUNK)
            y_rows.append(_dot_nt(m_y[cs], st) + y1[cs])
            psi = _dot_tn(tz[cs], bh_h[cs])
            gam = _dot_tn(u0[cs], bh_h[cs]) + _dot_tn(v_h[cs], kh_h[cs])
            st = st * pt_h[c * WKV_CHUNK:c * WKV_CHUNK + 1, :] + _dot_hi(st, psi) + gam
        state_s[h] = st
        ys.append(jnp.concatenate(y_rows, axis=0))
    y = jnp.concatenate(ys, axis=1)

    mean = _dot_split_lhs(y, hones, 2) * (1.0 / hd)
    yc = y - mean
    var = _dot_split_lhs(yc * yc, hones, 2) * (1.0 / hd)
    y = yc * lax.rsqrt(var + GN_EPS) * lnw_ref[...] + lnb_ref[...]
    y_a = (y + bonus) * g

    c0 = RWKV_SHIFT_COLS
    xc = cb_ref[...]
    for j in range(LRU_CONV):
        xc = xc + cw_ref[j:j + 1, :] * p_s[pl.ds(HALO - LRU_CONV + 1 + j, tm), c0:c0 + LRU_WIDTH]
    bg = p_s[pl.ds(HALO, tm), c0 + LRU_WIDTH:c0 + 2 * LRU_WIDTH]
    gr = jax.nn.sigmoid(_dot(xc, gaw_ref[...]) + gab_ref[...])
    gi = jax.nn.sigmoid(_dot(xc, gxw_ref[...]) + gxb_ref[...])
    log_a = -LRU_C * gr * jax.nn.softplus(-lam_ref[...])
    sa = jnp.exp(log_a)
    su = xc * gi * jnp.sqrt(1.0 - jnp.exp(2.0 * log_a))
    d = 1
    while d < tm:
        su = su + sa * _shift_rows(su, d, 0.0)
        sa = sa * _shift_rows(sa, d, 1.0)
        d *= 2
    hcar = hcar_s[0:1, :]
    hseq = su + sa * hcar
    hcar_s[...] = jnp.broadcast_to(hseq[tm - 1:tm, :], hcar_s.shape)
    y_b = hseq * jax.nn.gelu(bg)

    y_cat = jnp.concatenate([y_a, y_b], axis=1).astype(BF16)
    o_ref[...] = x + jnp.dot(y_cat, wout_ref[...], preferred_element_type=F32)


def _block_diag(w):
    h, n, _ = w.shape
    eye = jnp.eye(h, dtype=w.dtype)
    return (eye[:, None, :, None] * w[:, :, None, :]).reshape(h * n, h * n)


def _row(v):
    return v.reshape(1, -1).astype(F32)


def _const_spec(shape):
    nd = len(shape)
    return pl.BlockSpec(shape, lambda *_: (0,) * nd)


def _even_layer(x2, bsz, seq, norm_g, w_in, mu, w0, w2, a0, a2, g2, k_k, k_a, r_k, ln_w, ln_b,
                conv_w, conv_b, gate_a_w, gate_a_b, gate_x_w, gate_x_b, lru_lambda, w_out):
    tm = TM_MIX
    nt = seq // tm
    rw = RWKV_WIDTH
    wwa = jnp.zeros((W_LORA + A_LORA, 2 * rw), F32)
    wwa = wwa.at[:W_LORA, :rw].set(w2).at[W_LORA:, rw:].set(a2).astype(BF16)
    hones = _block_diag(jnp.ones((RWKV_HEADS, RWKV_HEAD, RWKV_HEAD), F32)).astype(BF16)
    consts = [
        _row(norm_g), w_in.astype(BF16), _row(mu), _row(w0), wwa, _row(a0), g2.astype(BF16),
        _row(k_k), _row(k_a), _row(r_k), _row(ln_w), _row(ln_b), conv_w.astype(F32), _row(conv_b),
        _block_diag(gate_a_w).astype(BF16), _row(gate_a_b), _block_diag(gate_x_w).astype(BF16),
        _row(gate_x_b), _row(lru_lambda), hones, w_out.astype(BF16),
    ]
    hb = tm // HALO
    in_specs = [
        pl.BlockSpec((tm, D_MODEL), lambda b, i: (b * nt + i, 0)),
        pl.BlockSpec((HALO, D_MODEL), lambda b, i: (jnp.maximum((b * nt + i) * hb - 1, 0), 0)),
    ] + [_const_spec(c.shape) for c in consts]
    return pl.pallas_call(
        functools.partial(_even_body, tm=tm),
        grid=(bsz, nt),
        in_specs=in_specs,
        out_specs=pl.BlockSpec((tm, D_MODEL), lambda b, i: (b * nt + i, 0)),
        out_shape=jax.ShapeDtypeStruct(x2.shape, F32),
        scratch_shapes=[
            pltpu.VMEM((tm + HALO, EVEN_IN_COLS), F32),
            pltpu.VMEM((RWKV_HEADS, RWKV_HEAD, RWKV_HEAD), F32),
            pltpu.VMEM((SUBLANES, LRU_WIDTH), F32),
        ],
        compiler_params=pltpu.CompilerParams(
            dimension_semantics=("arbitrary", "arbitrary"), vmem_limit_bytes=VMEM_LIMIT),
        name="even_mixer",
    )(x2, x2, *consts)


def _s5_param_body(are_ref, aim_ref, ldt_ref, bre_ref, bim_ref, pre_ref, pim_ref, bbre_ref, bbim_ref):
    lam_re = jnp.minimum(are_ref[...], -1e-4)
    lam_im = aim_ref[...]
    dt = jnp.exp(ldt_ref[...])
    mag = jnp.exp(lam_re * dt)
    ab_re = mag * jnp.cos(lam_im * dt)
    ab_im = mag * jnp.sin(lam_im * dt)
    den = lam_re * lam_re + lam_im * lam_im
    zr = ab_re - 1.0
    q_re = (zr * lam_re + ab_im * lam_im) / den
    q_im = (ab_im * lam_re - zr * lam_im) / den
    for c in range(S5_GROUP):
        bbre_ref[c] = q_re * bre_ref[c] - q_im * bim_ref[c]
        bbim_ref[c] = q_re * bim_ref[c] + q_im * bre_ref[c]
    for kp in range(SUBLANES):
        kf = float(kp + 1)
        mk = jnp.exp(lam_re * dt * kf)
        pre_ref[kp] = mk * jnp.cos(lam_im * dt * kf)
        pim_ref[kp] = mk * jnp.sin(lam_im * dt * kf)


def _s5_params(a_re, a_im, log_dt, b_re, b_im):
    g, n = S5_GROUPS, S5_STATE
    shp = jax.ShapeDtypeStruct
    return pl.pallas_call(
        _s5_param_body,
        out_shape=(shp((SUBLANES, g, n), F32), shp((SUBLANES, g, n), F32),
                   shp((S5_GROUP, g, n), F32), shp((S5_GROUP, g, n), F32)),
        name="s5_params",
    )(a_re.astype(F32), a_im.astype(F32), log_dt.reshape(g, 1).astype(F32),
      jnp.transpose(b_re, (2, 0, 1)).astype(F32), jnp.transpose(b_im, (2, 0, 1)).astype(F32))


def _odd_body(x_ref, ng_ref, win_ref, bre_ref, bim_ref, cre_ref, cim_ref, coef_ref, d_ref,
              wglu_ref, o_ref, sre_s, sim_s, car_s, *, tm):
    i = pl.program_id(1)
    nstrip = sre_s.shape[0]
    nblk = bre_ref.shape[0]
    per = nstrip // nblk
    cw = bre_ref.shape[1]

    @pl.when(i == 0)
    def _():
        car_s[...] = jnp.zeros_like(car_s)

    x = x_ref[...]
    u = jnp.dot(_rms(x, ng_ref[...]).astype(BF16), win_ref[...], preferred_element_type=F32)
    ub = u.astype(BF16)
    for j in range(nblk):
        uj = ub[:, j * cw:(j + 1) * cw]
        pr = jnp.dot(uj, bre_ref[j], preferred_element_type=F32)
        pi = jnp.dot(uj, bim_ref[j], preferred_element_type=F32)
        for q in range(per):
            sre_s[j * per + q] = pr[:, q * S5_LANES:(q + 1) * S5_LANES]
            sim_s[j * per + q] = pi[:, q * S5_LANES:(q + 1) * S5_LANES]

    def strip_body(s, _):
        cf = coef_ref[s]
        a1r, a1i, a2r, a2i, a4r, a4i, apr, api = [cf[m] for m in range(8)]

        def group_body(rg, carry):
            cr, ci = carry
            r0 = pl.multiple_of(rg * SUBLANES, SUBLANES)
            xr = sre_s[s, pl.ds(r0, SUBLANES), :]
            xi = sim_s[s, pl.ds(r0, SUBLANES), :]
            for dd, ar, ai in ((1, a1r, a1i), (2, a2r, a2i), (4, a4r, a4i)):
                sr = pltpu.roll(xr, dd, 0)
                si = pltpu.roll(xi, dd, 0)
                xr, xi = xr + ar * sr - ai * si, xi + ar * si + ai * sr
            xr, xi = xr + apr * cr - api * ci, xi + apr * ci + api * cr
            sre_s[s, pl.ds(r0, SUBLANES), :] = xr
            sim_s[s, pl.ds(r0, SUBLANES), :] = xi
            return (jnp.broadcast_to(xr[SUBLANES - 1:SUBLANES, :], xr.shape),
                    jnp.broadcast_to(xi[SUBLANES - 1:SUBLANES, :], xi.shape))

        cr, ci = lax.fori_loop(0, tm // SUBLANES, group_body, (car_s[0, s], car_s[1, s]))
        car_s[0, s] = cr
        car_s[1, s] = ci
        return 0

    lax.fori_loop(0, nstrip, strip_body, 0)

    ys = []
    for j in range(nblk):
        acc = None
        for q in range(per):
            rows = slice(q * S5_LANES, (q + 1) * S5_LANES)
            t = (_dot(sre_s[j * per + q], cre_ref[j, rows, :])
                 - _dot(sim_s[j * per + q], cim_ref[j, rows, :]))
            acc = t if acc is None else acc + t
        ys.append(acc)
    y = jnp.concatenate(ys, axis=1)
    yy = jax.nn.gelu(y + d_ref[...] * u).astype(BF16)
    og = jnp.dot(yy, wglu_ref[...], preferred_element_type=F32)
    o_ref[...] = x + og[:, 0:D_MODEL] * jax.nn.sigmoid(og[:, D_MODEL:2 * D_MODEL])


def _odd_layer(x2, bsz, seq, norm_g, w_in, a_re, a_im, log_dt, b_re, b_im, c_re, c_im, d_skip, w_glu):
    tm = TM_MIX
    nt = seq // tm
    g, n, gc = S5_GROUPS, S5_STATE, S5_GROUP
    gpb = 8
    nblk = g // gpb
    nstate = g * n
    nstrip = nstate // S5_LANES
    pw_re, pw_im, bb_re, bb_im = _s5_params(a_re, a_im, log_dt, b_re, b_im)
    eye = jnp.eye(gpb, dtype=F32)

    def b_blocks(bb):
        t = bb.reshape(gc, nblk, gpb, n)
        t = jnp.einsum('cjgn,gh->jgchn', t, eye)
        return t.reshape(nblk, gpb * gc, gpb * n).astype(BF16)

    def c_blocks(cc):
        t = cc.astype(F32).reshape(nblk, gpb, gc, n)
        t = jnp.einsum('jgcn,gh->jgnhc', t, eye)
        return t.reshape(nblk, gpb * n, gpb * gc).astype(BF16)

    def strips(p):
        return p.reshape(nstrip, S5_LANES)

    rows = jnp.arange(SUBLANES)[None, :, None]

    def masked(kp, lo):
        return (jnp.where(rows >= lo, strips(pw_re[kp - 1])[:, None, :], 0.0),
                jnp.where(rows >= lo, strips(pw_im[kp - 1])[:, None, :], 0.0))

    a1r, a1i = masked(1, 1)
    a2r, a2i = masked(2, 2)
    a4r, a4i = masked(4, 4)
    apr = jnp.transpose(pw_re.reshape(SUBLANES, nstrip, S5_LANES), (1, 0, 2))
    api = jnp.transpose(pw_im.reshape(SUBLANES, nstrip, S5_LANES), (1, 0, 2))
    coef = jnp.stack([a1r, a1i, a2r, a2i, a4r, a4i, apr, api], axis=1)

    consts = [_row(norm_g), w_in.astype(BF16), b_blocks(bb_re), b_blocks(bb_im),
              c_blocks(c_re), c_blocks(c_im), coef, _row(d_skip), w_glu.astype(BF16)]
    in_specs = [pl.BlockSpec((tm, D_MODEL), lambda b, i: (b * nt + i, 0))]
    in_specs += [_const_spec(c.shape) for c in consts]
    return pl.pallas_call(
        functools.partial(_odd_body, tm=tm),
        grid=(bsz, nt),
        in_specs=in_specs,
        out_specs=pl.BlockSpec((tm, D_MODEL), lambda b, i: (b * nt + i, 0)),
        out_shape=jax.ShapeDtypeStruct(x2.shape, F32),
        scratch_shapes=[
            pltpu.VMEM((nstrip, tm, S5_LANES), F32),
            pltpu.VMEM((nstrip, tm, S5_LANES), F32),
            pltpu.VMEM((2, nstrip, SUBLANES, S5_LANES), F32),
        ],
        compiler_params=pltpu.CompilerParams(
            dimension_semantics=("arbitrary", "arbitrary"), vmem_limit_bytes=VMEM_LIMIT),
        name="odd_mixer",
    )(x2, *consts)


def _ffn_body(x_ref, halo_ref, ng_ref, wg_ref, wv_ref, cwg_ref, cwv_ref, cbg_ref, cbv_ref,
              wd_ref, fg_ref, o_ref, xn_s, hg_s, hv_s, acc_s, *, tm, tiles_per_seq, final_norm):
    i = pl.program_id(0)
    j = pl.program_id(1)

    @pl.when(j == 0)
    def _():
        ng = ng_ref[...]
        halo = jnp.where(i % tiles_per_seq == 0, 0.0, halo_ref[...])
        xn_s[0:HALO, :] = _rms(halo, ng).astype(BF16)
        xn_s[HALO:HALO + tm, :] = _rms(x_ref[...], ng).astype(BF16)
        acc_s[...] = jnp.zeros_like(acc_s)

    xn = xn_s[...]
    hg_s[...] = jnp.dot(xn, wg_ref[...], preferred_element_type=F32)
    hv_s[...] = jnp.dot(xn, wv_ref[...], preferred_element_type=F32)

    def conv(h_s, cw_ref, cb_ref):
        out = cb_ref[...]
        for t in range(FFN_CONV):
            out = out + cw_ref[t:t + 1, :] * h_s[pl.ds(HALO - FFN_CONV + 1 + t, tm), :]
        return out

    gate = conv(hg_s, cwg_ref, cbg_ref)
    val = conv(hv_s, cwv_ref, cbv_ref)
    act = (jax.nn.silu(gate) * val).astype(BF16)
    acc_s[...] += jnp.dot(act, wd_ref[...], preferred_element_type=F32)

    @pl.when(j == pl.num_programs(1) - 1)
    def _():
        y = x_ref[...] + acc_s[...]
        if final_norm:
            y = _rms(y, fg_ref[...])
        o_ref[...] = y


def _ffn_layer(x2, seq, norm_g, w_up, conv_w, conv_b, w_down, final_g, final_norm):
    tm, tf = TM_FFN, TF_FFN
    m = x2.shape[0]
    nj = D_FF // tf
    hb = tm // HALO
    w_up = w_up.astype(BF16)
    conv_w = conv_w.astype(F32)
    conv_b = _row(conv_b)
    in_specs = [
        pl.BlockSpec((tm, D_MODEL), lambda i, j: (i, 0)),
        pl.BlockSpec((HALO, D_MODEL), lambda i, j: (jnp.maximum(i * hb - 1, 0), 0)),
        _const_spec((1, D_MODEL)),
        pl.BlockSpec((D_MODEL, tf), lambda i, j: (0, j)),
        pl.BlockSpec((D_MODEL, tf), lambda i, j: (0, nj + j)),
        pl.BlockSpec((FFN_CONV, tf), lambda i, j: (0, j)),
        pl.BlockSpec((FFN_CONV, tf), lambda i, j: (0, nj + j)),
        pl.BlockSpec((1, tf), lambda i, j: (0, j)),
        pl.BlockSpec((1, tf), lambda i, j: (0, nj + j)),
        pl.BlockSpec((tf, D_MODEL), lambda i, j: (j, 0)),
        _const_spec((1, D_MODEL)),
    ]
    return pl.pallas_call(
        functools.partial(_ffn_body, tm=tm, tiles_per_seq=seq // tm, final_norm=final_norm),
        grid=(m // tm, nj),
        in_specs=in_specs,
        out_specs=pl.BlockSpec((tm, D_MODEL), lambda i, j: (i, 0)),
        out_shape=jax.ShapeDtypeStruct(x2.shape, F32),
        scratch_shapes=[
            pltpu.VMEM((tm + HALO, D_MODEL), BF16),
            pltpu.VMEM((tm + HALO, tf), F32),
            pltpu.VMEM((tm + HALO, tf), F32),
            pltpu.VMEM((tm, D_MODEL), F32),
        ],
        compiler_params=pltpu.CompilerParams(
            dimension_semantics=("arbitrary", "arbitrary"), vmem_limit_bytes=VMEM_LIMIT),
        name="conv_ffn",
    )(x2, x2, _row(norm_g), w_up, w_up, conv_w, conv_w, conv_b, conv_b, w_down.astype(BF16),
      _row(final_g))


def kernel(x, e_norm_g, e_w_in, e_mu, e_w0, e_w2, e_a0, e_a2, e_g2, e_k_k, e_k_a, e_r_k,
           e_ln_w, e_ln_b, e_conv_w, e_conv_b, e_gate_a_w, e_gate_a_b, e_gate_x_w, e_gate_x_b,
           e_lru_lambda, e_w_out, o_norm_g, o_w_in, o_A_re, o_A_im, o_log_dt, o_B_re, o_B_im,
           o_C_re, o_C_im, o_D, o_w_glu, f_norm_g, f_w_up, f_conv_w, f_conv_b, f_w_down,
           final_norm_g):
    bsz, seq, dm = x.shape
    assert dm == D_MODEL and seq % TM_FFN == 0 and seq % TM_MIX == 0
    depth = f_norm_g.shape[0]
    h = x.reshape(bsz * seq, dm).astype(F32)
    for i in range(depth):
        j = i // 2
        if i % 2 == 0:
            h = _even_layer(h, bsz, seq, e_norm_g[j], e_w_in[j], e_mu[j], e_w0[j], e_w2[j],
                            e_a0[j], e_a2[j], e_g2[j], e_k_k[j], e_k_a[j], e_r_k[j], e_ln_w[j],
                            e_ln_b[j], e_conv_w[j], e_conv_b[j], e_gate_a_w[j], e_gate_a_b[j],
                            e_gate_x_w[j], e_gate_x_b[j], e_lru_lambda[j], e_w_out[j])
        else:
            h = _odd_layer(h, bsz, seq, o_norm_g[j], o_w_in[j], o_A_re[j], o_A_im[j],
                           o_log_dt[j], o_B_re[j], o_B_im[j], o_C_re[j], o_C_im[j], o_D[j],
                           o_w_glu[j])
        h = _ffn_layer(h, seq, f_norm_g[i], f_w_up[i], f_conv_w[i], f_conv_b[i], f_w_down[i],
                       final_norm_g, final_norm=(i == depth - 1))
    return h.reshape(bsz, seq, dm).astype(x.dtype)
```

```python
import functools

import jax
import jax.numpy as jnp
from jax import lax
from jax.experimental import pallas as pl
from jax.experimental.pallas import tpu as pltpu

F32 = jnp.float32
BF16 = jnp.bfloat16

D_MODEL = 1024
RWKV_WIDTH = 512
RWKV_HEAD = 64
RWKV_HEADS = 8
W_LORA = 64
A_LORA = 64
G_LORA = 128
LRU_WIDTH = 512
LRU_BLOCKS = 8
LRU_BLOCK = 64
LRU_CONV = 4
LRU_C = 8.0
S5_WIDTH = 1024
S5_GROUP = 16
S5_GROUPS = 64
S5_STATE = 64
D_FF = 2816
FFN_CONV = 3
NORM_EPS = 1e-6
GN_EPS = 64e-5
RWKV_SHIFT_COLS = 3 * RWKV_WIDTH + W_LORA + A_LORA + G_LORA
EVEN_IN_COLS = RWKV_SHIFT_COLS + 2 * LRU_WIDTH

SUBLANES = 8
HALO = 16
TM_MIX = 256
WKV_CHUNK = 64
TT_ODD = 32
S5_LANES = 256
TM_FFN = 512
TF_FFN = 256
VMEM_LIMIT = 56 * 1024 * 1024


def _dot(a, b):
    return jnp.dot(a.astype(BF16), b.astype(BF16), preferred_element_type=F32)


def _dot_nt(a, b):
    return lax.dot_general(a.astype(BF16), b.astype(BF16), (((1,), (1,)), ((), ())),
                           preferred_element_type=F32)


def _split(x, n):
    parts = []
    for _ in range(n - 1):
        h = x.astype(BF16)
        parts.append(h)
        x = x - h.astype(F32)
    parts.append(x.astype(BF16))
    return parts


def _dot_split_rhs(a_bf16, b, n):
    acc = None
    for piece in _split(b, n):
        t = jnp.dot(a_bf16, piece, preferred_element_type=F32)
        acc = t if acc is None else acc + t
    return acc


def _rms(x, g):
    ms = jnp.mean(x * x, axis=-1, keepdims=True)
    return x * lax.rsqrt(ms + NORM_EPS) * g


def _shift_rows(x, d, fill):
    rows = lax.broadcasted_iota(jnp.int32, x.shape, 0)
    return jnp.where(rows >= d, pltpu.roll(x, d, 0), fill)


def _even_body(x_ref, halo_ref, ng_ref, win_ref, mu_ref, w0_ref, wwa_ref, a0_ref, g2_ref,
               kk_ref, ka_ref, rk_ref, lnw_ref, lnb_ref, cw_ref, cb_ref, gaw_ref, gab_ref,
               gxw_ref, gxb_ref, lam_ref, hones_ref, wout_ref, o_ref,
               p_s, state_s, hcar_s, *, tm):
    i = pl.program_id(1)
    rw = RWKV_WIDTH
    hd = RWKV_HEAD
    lc = WKV_CHUNK
    nchunk = tm // lc

    @pl.when(i == 0)
    def _():
        state_s[...] = jnp.zeros_like(state_s)
        hcar_s[...] = jnp.zeros_like(hcar_s)

    x = x_ref[...]
    halo = jnp.where(i == 0, 0.0, halo_ref[...])
    ng = ng_ref[...]
    xn = jnp.concatenate([_rms(halo, ng), _rms(x, ng)], axis=0).astype(BF16)
    p_s[...] = jnp.dot(xn, win_ref[...], preferred_element_type=F32)

    pa = p_s[pl.ds(HALO, tm), 0:RWKV_SHIFT_COLS]
    pp = p_s[pl.ds(HALO - 1, tm), 0:RWKV_SHIFT_COLS]
    pm = pa + mu_ref[...] * (pp - pa)
    r = pm[:, 0:rw]
    k = pm[:, rw:2 * rw]
    v = pm[:, 2 * rw:3 * rw]
    wa = pm[:, 3 * rw:3 * rw + W_LORA + A_LORA]
    gd = pm[:, 3 * rw + W_LORA + A_LORA:RWKV_SHIFT_COLS]
    lane = lax.broadcasted_iota(jnp.int32, wa.shape, 1)
    wa = jnp.where(lane < W_LORA, jnp.tanh(wa), wa)
    lora = _dot(wa, wwa_ref[...])
    w_log = -jax.nn.softplus(-(w0_ref[...] + lora[:, 0:rw])) - 0.5
    logw = -jnp.exp(w_log)
    a = jax.nn.sigmoid(a0_ref[...] + lora[:, rw:2 * rw])
    g = _dot(jax.nn.sigmoid(gd), g2_ref[...])
    hones = hones_ref[...]
    kk = k * kk_ref[...]
    kk = kk / jnp.maximum(jnp.sqrt(_dot(kk * kk, hones)), 1e-12)
    k = k * (1.0 + (a - 1.0) * ka_ref[...])
    bonus = _dot(r * k * rk_ref[...], hones) * v
    zv = -kk
    bv = kk * a

    row = lax.broadcasted_iota(jnp.int32, (tm, tm), 0)
    col = lax.broadcasted_iota(jnp.int32, (tm, tm), 1)
    same = (row // lc) == (col // lc)
    incl = same & (row >= col)
    strict = same & (row > col)
    cum = _dot_split_rhs(incl.astype(BF16), logw, 3)
    tot = jnp.concatenate(
        [jnp.broadcast_to(cum[(c + 1) * lc - 1:(c + 1) * lc, :], (lc, rw)) for c in range(nchunk)],
        axis=0)
    inv_p = jnp.exp(-cum)
    to_end = jnp.exp(tot - cum)
    rt = r * jnp.exp(cum)
    zt = zv * jnp.exp(cum - logw)
    kt = k * inv_p
    bt = bv * inv_p
    kh_t = (k * to_end).T
    bh_t = (bv * to_end).T
    pt_t = jnp.exp(tot).T
    eye = (row == col).astype(F32)

    heads = range(RWKV_HEADS)
    hsl = [slice(h * hd, (h + 1) * hd) for h in heads]
    z_h = [zt[:, s] for s in hsl]
    r_h = [rt[:, s] for s in hsl]
    v_h = [v[:, s] for s in hsl]
    n_zb, a_zk, a_rb, a_rk = [], [], [], []
    for h in heads:
        zr = jnp.concatenate([z_h[h], r_h[h]], axis=0)
        bk = jnp.concatenate([bt[:, hsl[h]], kt[:, hsl[h]]], axis=0)
        aa = _dot_nt(zr, bk)
        n_zb.append(jnp.where(strict, aa[0:tm, 0:tm], 0.0))
        a_zk.append(jnp.where(strict, aa[0:tm, tm:2 * tm], 0.0))
        a_rb.append(jnp.where(incl, aa[tm:2 * tm, 0:tm], 0.0))
        a_rk.append(jnp.where(incl, aa[tm:2 * tm, tm:2 * tm], 0.0))
    xp = list(n_zb)
    t_inv = [eye + n for n in n_zb]
    span = 2
    while span < lc:
        xp = [_dot(m, m) for m in xp]
        t_inv = [t + _dot(t, m) for t, m in zip(t_inv, xp)]
        span *= 2
    wv = [_dot(a_zk[h], v_h[h]) for h in heads]
    tzu = [_dot(t_inv[h], jnp.concatenate([z_h[h], wv[h]], axis=1)) for h in heads]
    ar = [_dot(a_rb[h], tzu[h]) for h in heads]
    m_y = [r_h[h] + ar[h][:, 0:hd] for h in heads]
    y1 = [ar[h][:, hd:2 * hd] + _dot(a_rk[h], v_h[h]) for h in heads]
    phi, gam = [], []
    for h in heads:
        b_blk = jnp.where(same, jnp.concatenate([bh_t[hsl[h], :]] * nchunk, axis=0), 0.0)
        k_blk = jnp.where(same, jnp.concatenate([kh_t[hsl[h], :]] * nchunk, axis=0), 0.0)
        pg = _dot(b_blk, tzu[h])
        phi.append(pg[:, 0:hd])
        gam.append(pg[:, hd:2 * hd] + _dot(k_blk, v_h[h]))
    st = [state_s[h] for h in heads]
    y_rows = [[] for _ in heads]
    for c in range(nchunk):
        cs = slice(c * lc, (c + 1) * lc)
        for h in heads:
            y_rows[h].append(_dot(m_y[h][cs], st[h]) + y1[h][cs])
            decay = pt_t[hsl[h], c * lc:c * lc + 1]
            st[h] = st[h] * decay + _dot(phi[h][cs], st[h]) + gam[h][cs]
    for h in heads:
        state_s[h] = st[h]
    y = jnp.concatenate([jnp.concatenate(rows, axis=0) for rows in y_rows], axis=1)

    mean = _dot(y, hones) * (1.0 / hd)
    yc = y - mean
    var = _dot(yc * yc, hones) * (1.0 / hd)
    y = yc * lax.rsqrt(var + GN_EPS) * lnw_ref[...] + lnb_ref[...]
    y_a = (y + bonus) * g

    c0 = RWKV_SHIFT_COLS
    xc = cb_ref[...]
    for j in range(LRU_CONV):
        xc = xc + cw_ref[j:j + 1, :] * p_s[pl.ds(HALO - LRU_CONV + 1 + j, tm), c0:c0 + LRU_WIDTH]
    bg = p_s[pl.ds(HALO, tm), c0 + LRU_WIDTH:c0 + 2 * LRU_WIDTH]
    gr = jax.nn.sigmoid(_dot(xc, gaw_ref[...]) + gab_ref[...])
    gi = jax.nn.sigmoid(_dot(xc, gxw_ref[...]) + gxb_ref[...])
    log_a = -LRU_C * gr * jax.nn.softplus(-lam_ref[...])
    sa = jnp.exp(log_a)
    su = xc * gi * jnp.sqrt(1.0 - jnp.exp(2.0 * log_a))
    d = 1
    while d < tm:
        su = su + sa * _shift_rows(su, d, 0.0)
        sa = sa * _shift_rows(sa, d, 1.0)
        d *= 2
    hcar = hcar_s[0:1, :]
    hseq = su + sa * hcar
    hcar_s[...] = jnp.broadcast_to(hseq[tm - 1:tm, :], hcar_s.shape)
    y_b = hseq * jax.nn.gelu(bg)

    y_cat = jnp.concatenate([y_a, y_b], axis=1).astype(BF16)
    o_ref[...] = x + jnp.dot(y_cat, wout_ref[...], preferred_element_type=F32)


def _block_diag(w):
    h, n, _ = w.shape
    eye = jnp.eye(h, dtype=w.dtype)
    return (eye[:, None, :, None] * w[:, :, None, :]).reshape(h * n, h * n)


def _row(v):
    return v.reshape(1, -1).astype(F32)


def _const_spec(shape):
    nd = len(shape)
    return pl.BlockSpec(shape, lambda *_: (0,) * nd, pipeline_mode=pl.Buffered(1))


def _even_layer(x2, bsz, seq, norm_g, w_in, mu, w0, w2, a0, a2, g2, k_k, k_a, r_k, ln_w, ln_b,
                conv_w, conv_b, gate_a_w, gate_a_b, gate_x_w, gate_x_b, lru_lambda, w_out):
    tm = TM_MIX
    nt = seq // tm
    rw = RWKV_WIDTH
    wwa = jnp.zeros((W_LORA + A_LORA, 2 * rw), F32)
    wwa = wwa.at[:W_LORA, :rw].set(w2).at[W_LORA:, rw:].set(a2).astype(BF16)
    hones = _block_diag(jnp.ones((RWKV_HEADS, RWKV_HEAD, RWKV_HEAD), F32)).astype(BF16)
    consts = [
        _row(norm_g), w_in.astype(BF16), _row(mu), _row(w0), wwa, _row(a0), g2.astype(BF16),
        _row(k_k), _row(k_a), _row(r_k), _row(ln_w), _row(ln_b), conv_w.astype(F32), _row(conv_b),
        _block_diag(gate_a_w).astype(BF16), _row(gate_a_b), _block_diag(gate_x_w).astype(BF16),
        _row(gate_x_b), _row(lru_lambda), hones, w_out.astype(BF16),
    ]
    hb = tm // HALO
    in_specs = [
        pl.BlockSpec((tm, D_MODEL), lambda b, i: (b * nt + i, 0)),
        pl.BlockSpec((HALO, D_MODEL), lambda b, i: (jnp.maximum((b * nt + i) * hb - 1, 0), 0)),
    ] + [_const_spec(c.shape) for c in consts]
    return pl.pallas_call(
        functools.partial(_even_body, tm=tm),
        grid=(bsz, nt),
        in_specs=in_specs,
        out_specs=pl.BlockSpec((tm, D_MODEL), lambda b, i: (b * nt + i, 0)),
        out_shape=jax.ShapeDtypeStruct(x2.shape, F32),
        scratch_shapes=[
            pltpu.VMEM((tm + HALO, EVEN_IN_COLS), F32),
            pltpu.VMEM((RWKV_HEADS, RWKV_HEAD, RWKV_HEAD), F32),
            pltpu.VMEM((SUBLANES, LRU_WIDTH), F32),
        ],
        compiler_params=pltpu.CompilerParams(
            dimension_semantics=("arbitrary", "arbitrary"), vmem_limit_bytes=VMEM_LIMIT),
        name="even_mixer",
    )(x2, x2, *consts)


def _s5_param_body(are_ref, aim_ref, ldt_ref, bre_ref, bim_ref, abre_ref, abim_ref, bbre_ref, bbim_ref):
    lam_re = jnp.minimum(are_ref[...], -1e-4)
    lam_im = aim_ref[...]
    dt = jnp.exp(ldt_ref[...])
    mag = jnp.exp(lam_re * dt)
    ab_re = mag * jnp.cos(lam_im * dt)
    ab_im = mag * jnp.sin(lam_im * dt)
    den = lam_re * lam_re + lam_im * lam_im
    zr = ab_re - 1.0
    q_re = (zr * lam_re + ab_im * lam_im) / den
    q_im = (ab_im * lam_re - zr * lam_im) / den
    abre_ref[...] = ab_re
    abim_ref[...] = ab_im
    for c in range(S5_GROUP):
        bbre_ref[c] = q_re * bre_ref[c] - q_im * bim_ref[c]
        bbim_ref[c] = q_re * bim_ref[c] + q_im * bre_ref[c]


def _s5_params(a_re, a_im, log_dt, b_re, b_im):
    g, n = S5_GROUPS, S5_STATE
    shp = jax.ShapeDtypeStruct
    return pl.pallas_call(
        _s5_param_body,
        out_shape=(shp((g, n), F32), shp((g, n), F32),
                   shp((S5_GROUP, g, n), F32), shp((S5_GROUP, g, n), F32)),
        name="s5_params",
    )(a_re.astype(F32), a_im.astype(F32), log_dt.reshape(g, 1).astype(F32),
      jnp.transpose(b_re, (2, 0, 1)).astype(F32), jnp.transpose(b_im, (2, 0, 1)).astype(F32))


def _odd_body(x_ref, perm_ref, permt_ref, ng_ref, win_ref, bre_ref, bim_ref, cre_ref, cim_ref,
              are_ref, aim_ref, d_ref, wglu_ref, o_ref, sre_s, sim_s, car_s, *, tt):
    i = pl.program_id(1)
    rows = SUBLANES * tt
    nstrip = sre_s.shape[0]
    nblk = bre_ref.shape[0]
    per = nstrip // nblk
    cw = bre_ref.shape[1]

    @pl.when(i == 0)
    def _():
        car_s[...] = jnp.zeros_like(car_s)

    x = x_ref[...].reshape(rows, D_MODEL)
    xn = _rms(x, ng_ref[...]).astype(BF16)
    xn = jnp.dot(perm_ref[...], xn, preferred_element_type=F32).astype(BF16)
    u = jnp.dot(xn, win_ref[...], preferred_element_type=F32)
    ub = u.astype(BF16)
    for j in range(nblk):
        uj = ub[:, j * cw:(j + 1) * cw]
        pr = jnp.dot(uj, bre_ref[j], preferred_element_type=F32)
        pi = jnp.dot(uj, bim_ref[j], preferred_element_type=F32)
        for q in range(per):
            sre_s[j * per + q] = pr[:, q * S5_LANES:(q + 1) * S5_LANES]
            sim_s[j * per + q] = pi[:, q * S5_LANES:(q + 1) * S5_LANES]

    for s in range(nstrip):
        ar = are_ref[s]
        ai = aim_ref[s]
        xr = car_s[0, s]
        xi = car_s[1, s]
        for t in range(tt):
            rs = slice(t * SUBLANES, (t + 1) * SUBLANES)
            xr, xi = (ar * xr - ai * xi + sre_s[s, rs, :], ar * xi + ai * xr + sim_s[s, rs, :])
            sre_s[s, rs, :] = xr
            sim_s[s, rs, :] = xi
        car_s[0, s] = xr
        car_s[1, s] = xi

    ys = []
    for j in range(nblk):
        acc = None
        for q in range(per):
            rsl = slice(q * S5_LANES, (q + 1) * S5_LANES)
            t = (_dot(sre_s[j * per + q], cre_ref[j, rsl, :])
                 - _dot(sim_s[j * per + q], cim_ref[j, rsl, :]))
            acc = t if acc is None else acc + t
        ys.append(acc)
    y = jnp.concatenate(ys, axis=1)
    yy = jax.nn.gelu(y + d_ref[...] * u).astype(BF16)
    yy = jnp.dot(permt_ref[...], yy, preferred_element_type=F32).astype(BF16)
    og = jnp.dot(yy, wglu_ref[...], preferred_element_type=F32)
    out = x + og[:, 0:D_MODEL] * jax.nn.sigmoid(og[:, D_MODEL:2 * D_MODEL])
    o_ref[...] = out.reshape(o_ref.shape)


def _odd_layer(x2, bsz, seq, norm_g, w_in, a_re, a_im, log_dt, b_re, b_im, c_re, c_im, d_skip, w_glu):
    tt = TT_ODD
    assert bsz % SUBLANES == 0 and seq % tt == 0
    nt = seq // tt
    rows = SUBLANES * tt
    g, n, gc = S5_GROUPS, S5_STATE, S5_GROUP
    gpb = 8
    nblk = g // gpb
    nstrip = g * n // S5_LANES
    ab_re, ab_im, bb_re, bb_im = _s5_params(a_re, a_im, log_dt, b_re, b_im)
    eye = jnp.eye(gpb, dtype=F32)

    def b_blocks(bb):
        t = bb.reshape(gc, nblk, gpb, n)
        t = jnp.einsum('cjgn,gh->jgchn', t, eye)
        return t.reshape(nblk, gpb * gc, gpb * n).astype(BF16)

    def c_blocks(cc):
        t = cc.astype(F32).reshape(nblk, gpb, gc, n)
        t = jnp.einsum('jgcn,gh->jgnhc', t, eye)
        return t.reshape(nblk, gpb * n, gpb * gc).astype(BF16)

    def strips(p):
        return jnp.broadcast_to(p.reshape(nstrip, 1, S5_LANES), (nstrip, SUBLANES, S5_LANES))

    dst = jnp.arange(rows)
    src = (dst % SUBLANES) * tt + dst // SUBLANES
    perm = (src[:, None] == jnp.arange(rows)[None, :]).astype(BF16)

    consts = [perm, perm.T, _row(norm_g), w_in.astype(BF16), b_blocks(bb_re), b_blocks(bb_im),
              c_blocks(c_re), c_blocks(c_im), strips(ab_re), strips(ab_im), _row(d_skip),
              w_glu.astype(BF16)]
    x4 = x2.reshape(bsz, nt, tt, D_MODEL)
    xspec = pl.BlockSpec((SUBLANES, 1, tt, D_MODEL), lambda b, i: (b, i, 0, 0))
    out = pl.pallas_call(
        functools.partial(_odd_body, tt=tt),
        grid=(bsz // SUBLANES, nt),
        in_specs=[xspec] + [_const_spec(c.shape) for c in consts],
        out_specs=xspec,
        out_shape=jax.ShapeDtypeStruct(x4.shape, F32),
        scratch_shapes=[
            pltpu.VMEM((nstrip, rows, S5_LANES), F32),
            pltpu.VMEM((nstrip, rows, S5_LANES), F32),
            pltpu.VMEM((2, nstrip, SUBLANES, S5_LANES), F32),
        ],
        compiler_params=pltpu.CompilerParams(
            dimension_semantics=("arbitrary", "arbitrary"), vmem_limit_bytes=VMEM_LIMIT),
        name="odd_mixer",
    )(x4, *consts)
    return out.reshape(x2.shape)


def _ffn_body(x_ref, halo_ref, ng_ref, wup_ref, cw_ref, cb_ref, wd_ref, fg_ref, o_ref,
              h_s, act_s, *, tm, tiles_per_seq, final_norm):
    i = pl.program_id(0)
    tf = TF_FFN
    ng = ng_ref[...]
    x = x_ref[...]
    halo = jnp.where(i % tiles_per_seq == 0, 0.0, halo_ref[...])
    xn = jnp.concatenate([_rms(halo, ng), _rms(x, ng)], axis=0).astype(BF16)

    def conv(c0):
        out = cb_ref[:, c0:c0 + tf]
        for t in range(FFN_CONV):
            out = out + (cw_ref[t:t + 1, c0:c0 + tf]
                         * h_s[pl.ds(HALO - FFN_CONV + 1 + t, tm), c0:c0 + tf])
        return out

    for c in range(D_FF // tf):
        for c0 in (c * tf, D_FF + c * tf):
            h_s[:, c0:c0 + tf] = jnp.dot(xn, wup_ref[:, c0:c0 + tf], preferred_element_type=F32)
        act = jax.nn.silu(conv(c * tf)) * conv(D_FF + c * tf)
        act_s[:, c * tf:(c + 1) * tf] = act.astype(BF16)
    y = x + jnp.dot(act_s[...], wd_ref[...], preferred_element_type=F32)
    if final_norm:
        y = _rms(y, fg_ref[...])
    o_ref[...] = y


def _ffn_layer(x2, seq, norm_g, w_up, conv_w, conv_b, w_down, final_g, final_norm):
    tm = TM_FFN
    m = x2.shape[0]
    hb = tm // HALO
    consts = [_row(norm_g), w_up.astype(BF16), conv_w.astype(F32), _row(conv_b),
              w_down.astype(BF16), _row(final_g)]
    in_specs = [
        pl.BlockSpec((tm, D_MODEL), lambda i: (i, 0)),
        pl.BlockSpec((HALO, D_MODEL), lambda i: (jnp.maximum(i * hb - 1, 0), 0)),
    ] + [_const_spec(c.shape) for c in consts]
    return pl.pallas_call(
        functools.partial(_ffn_body, tm=tm, tiles_per_seq=seq // tm, final_norm=final_norm),
        grid=(m // tm,),
        in_specs=in_specs,
        out_specs=pl.BlockSpec((tm, D_MODEL), lambda i: (i, 0)),
        out_shape=jax.ShapeDtypeStruct(x2.shape, F32),
        scratch_shapes=[
            pltpu.VMEM((tm + HALO, 2 * D_FF), F32),
            pltpu.VMEM((tm, D_FF), BF16),
        ],
        compiler_params=pltpu.CompilerParams(
            dimension_semantics=("arbitrary",), vmem_limit_bytes=VMEM_LIMIT),
        name="conv_ffn",
    )(x2, x2, *consts)


def kernel(x, e_norm_g, e_w_in, e_mu, e_w0, e_w2, e_a0, e_a2, e_g2, e_k_k, e_k_a, e_r_k,
           e_ln_w, e_ln_b, e_conv_w, e_conv_b, e_gate_a_w, e_gate_a_b, e_gate_x_w, e_gate_x_b,
           e_lru_lambda, e_w_out, o_norm_g, o_w_in, o_A_re, o_A_im, o_log_dt, o_B_re, o_B_im,
           o_C_re, o_C_im, o_D, o_w_glu, f_norm_g, f_w_up, f_conv_w, f_conv_b, f_w_down,
           final_norm_g):
    bsz, seq, dm = x.shape
    assert dm == D_MODEL and seq % TM_FFN == 0 and seq % TM_MIX == 0
    depth = f_norm_g.shape[0]
    h = x.reshape(bsz * seq, dm).astype(F32)
    for i in range(depth):
        j = i // 2
        if i % 2 == 0:
            h = _even_layer(h, bsz, seq, e_norm_g[j], e_w_in[j], e_mu[j], e_w0[j], e_w2[j],
                            e_a0[j], e_a2[j], e_g2[j], e_k_k[j], e_k_a[j], e_r_k[j], e_ln_w[j],
                            e_ln_b[j], e_conv_w[j], e_conv_b[j], e_gate_a_w[j], e_gate_a_b[j],
                            e_gate_x_w[j], e_gate_x_b[j], e_lru_lambda[j], e_w_out[j])
        else:
            h = _odd_layer(h, bsz, seq, o_norm_g[j], o_w_in[j], o_A_re[j], o_A_im[j],
                           o_log_dt[j], o_B_re[j], o_B_im[j], o_C_re[j], o_C_im[j], o_D[j],
                           o_w_glu[j])
        h = _ffn_layer(h, seq, f_norm_g[i], f_w_up[i], f_conv_w[i], f_conv_b[i], f_w_down[i],
                       final_norm_g, final_norm=(i == depth - 1))
    return h.reshape(bsz, seq, dm).astype(x.dtype)
```

```python
import functools

import jax
import jax.numpy as jnp
from jax import lax
from jax.experimental import pallas as pl
from jax.experimental.pallas import tpu as pltpu

F32 = jnp.float32
BF16 = jnp.bfloat16

D_MODEL = 1024
RWKV_WIDTH = 512
RWKV_HEAD = 64
RWKV_HEADS = 8
W_LORA = 64
A_LORA = 64
G_LORA = 128
LRU_WIDTH = 512
LRU_BLOCKS = 8
LRU_BLOCK = 64
LRU_CONV = 4
LRU_C = 8.0
S5_WIDTH = 1024
S5_GROUP = 16
S5_GROUPS = 64
S5_STATE = 64
D_FF = 2816
FFN_CONV = 3
NORM_EPS = 1e-6
GN_EPS = 64e-5
RWKV_SHIFT_COLS = 3 * RWKV_WIDTH + W_LORA + A_LORA + G_LORA
EVEN_IN_COLS = RWKV_SHIFT_COLS + 2 * LRU_WIDTH

SUBLANES = 8
TM_MIX = 256
WKV_CHUNK = 64
TT_ODD = 32
S5_LANES = 256
TM_FFN = 512
TF_FFN = 256
VMEM_LIMIT = 56 * 1024 * 1024


def _dot(a, b):
    return jnp.dot(a.astype(BF16), b.astype(BF16), preferred_element_type=F32)


def _dot_nt(a, b):
    return lax.dot_general(a.astype(BF16), b.astype(BF16), (((1,), (1,)), ((), ())),
                           preferred_element_type=F32)


def _split(x, n):
    parts = []
    for _ in range(n - 1):
        h = x.astype(BF16)
        parts.append(h)
        x = x - h.astype(F32)
    parts.append(x.astype(BF16))
    return parts


def _dot_split_rhs(a_bf16, b, n):
    acc = None
    for piece in _split(b, n):
        t = jnp.dot(a_bf16, piece, preferred_element_type=F32)
        acc = t if acc is None else acc + t
    return acc


def _rms(x, g):
    ms = jnp.mean(x * x, axis=-1, keepdims=True)
    return x * lax.rsqrt(ms + NORM_EPS) * g


def _half_dot(x, w_ref):
    half = x.shape[1] // 2
    return jnp.concatenate([_dot(x[:, 0:half], w_ref[0]), _dot(x[:, half:2 * half], w_ref[1])], axis=1)


def _even_body(x_ref, ng_ref, win_ref, mu_ref, w0_ref, wwa_ref, a0_ref, g2_ref,
               kk_ref, ka_ref, rk_ref, lnw_ref, lnb_ref, cw_ref, cb_ref, gaw_ref, gab_ref,
               gxw_ref, gxb_ref, lam_ref, hones_ref, wout_ref, o_ref,
               p_s, state_s, hcar_s, *, tm):
    i = pl.program_id(1)
    rw = RWKV_WIDTH
    hd = RWKV_HEAD
    lc = WKV_CHUNK
    nchunk = tm // lc

    @pl.when(i == 0)
    def _():
        state_s[...] = jnp.zeros_like(state_s)
        hcar_s[...] = jnp.zeros_like(hcar_s)
        p_s[...] = jnp.zeros_like(p_s)

    x = x_ref[...]
    xn = _rms(x, ng_ref[...]).astype(BF16)
    p = jnp.dot(xn, win_ref[...], preferred_element_type=F32)
    p_full = jnp.concatenate([p_s[...], p], axis=0)
    p_s[...] = p[tm - SUBLANES:tm, :]

    def lagged(c_lo, c_hi, lag):
        return p_full[SUBLANES - lag:SUBLANES - lag + tm, c_lo:c_hi]

    pa = p[:, 0:RWKV_SHIFT_COLS]
    pp = lagged(0, RWKV_SHIFT_COLS, 1)
    pm = pa + mu_ref[...] * (pp - pa)
    r = pm[:, 0:rw]
    k = pm[:, rw:2 * rw]
    v = pm[:, 2 * rw:3 * rw]
    wa = pm[:, 3 * rw:3 * rw + W_LORA + A_LORA]
    gd = pm[:, 3 * rw + W_LORA + A_LORA:RWKV_SHIFT_COLS]
    lane = lax.broadcasted_iota(jnp.int32, wa.shape, 1)
    wa = jnp.where(lane < W_LORA, jnp.tanh(wa), wa)
    lora = _dot(wa, wwa_ref[...])
    w_log = -jax.nn.softplus(-(w0_ref[...] + lora[:, 0:rw])) - 0.5
    logw = -jnp.exp(w_log)
    a = jax.nn.sigmoid(a0_ref[...] + lora[:, rw:2 * rw])
    g = _dot(jax.nn.sigmoid(gd), g2_ref[...])
    kk = k * kk_ref[...]
    kk = kk * jnp.minimum(lax.rsqrt(_half_dot(kk * kk, hones_ref)), 1e12)
    k = k * (1.0 + (a - 1.0) * ka_ref[...])
    bonus = _half_dot(r * k * rk_ref[...], hones_ref) * v
    zv = -kk
    bv = kk * a

    row = lax.broadcasted_iota(jnp.int32, (tm, tm), 0)
    col = lax.broadcasted_iota(jnp.int32, (tm, tm), 1)
    same = (row // lc) == (col // lc)
    incl = same & (row >= col)
    strict = same & (row > col)
    cum = _dot_split_rhs(incl.astype(BF16), logw, 3)
    tot = jnp.concatenate(
        [jnp.broadcast_to(cum[(c + 1) * lc - 1:(c + 1) * lc, :], (lc, rw)) for c in range(nchunk)],
        axis=0)
    inv_p = jnp.exp(-cum)
    to_end = jnp.exp(tot - cum)
    rt = r * jnp.exp(cum)
    zt = zv * jnp.exp(cum - logw)
    kt = k * inv_p
    bt = bv * inv_p
    kh_t = (k * to_end).T
    bh_t = (bv * to_end).T
    pt_t = jnp.exp(tot).T
    eye = (row == col).astype(F32)

    heads = range(RWKV_HEADS)
    hsl = [slice(h * hd, (h + 1) * hd) for h in heads]
    z_h = [zt[:, s] for s in hsl]
    r_h = [rt[:, s] for s in hsl]
    v_h = [v[:, s] for s in hsl]
    n_zb, a_zk, a_rb, a_rk = [], [], [], []
    for h in heads:
        zr = jnp.concatenate([z_h[h], r_h[h]], axis=0)
        bk = jnp.concatenate([bt[:, hsl[h]], kt[:, hsl[h]]], axis=0)
        aa = _dot_nt(zr, bk)
        n_zb.append(jnp.where(strict, aa[0:tm, 0:tm], 0.0))
        a_zk.append(jnp.where(strict, aa[0:tm, tm:2 * tm], 0.0))
        a_rb.append(jnp.where(incl, aa[tm:2 * tm, 0:tm], 0.0))
        a_rk.append(jnp.where(incl, aa[tm:2 * tm, tm:2 * tm], 0.0))
    c0 = RWKV_SHIFT_COLS
    xc = cb_ref[...]
    for j in range(LRU_CONV):
        xc = xc + cw_ref[j:j + 1, :] * lagged(c0, c0 + LRU_WIDTH, LRU_CONV - 1 - j)
    gate_a = _half_dot(xc, gaw_ref) + gab_ref[...]
    gate_x = _half_dot(xc, gxw_ref) + gxb_ref[...]
    neg_c_softplus = -LRU_C * jax.nn.softplus(-lam_ref[...])
    piece = tm // RWKV_HEADS
    in_group = lax.broadcasted_iota(jnp.int32, (piece, LRU_WIDTH), 0) % SUBLANES

    def lru_piece(q):
        rs = slice(q * piece, (q + 1) * piece)
        log_a = neg_c_softplus * jax.nn.sigmoid(gate_a[rs])
        sa = jnp.exp(log_a)
        su = xc[rs] * jax.nn.sigmoid(gate_x[rs]) * jnp.sqrt(1.0 - jnp.exp(2.0 * log_a))
        for d in (1, 2, 4):
            keep = in_group >= d
            su, sa = (su + sa * jnp.where(keep, pltpu.roll(su, d, 0), 0.0),
                      sa * jnp.where(keep, pltpu.roll(sa, d, 0), 1.0))
        return su, sa, jax.nn.gelu(p[rs, c0 + LRU_WIDTH:c0 + 2 * LRU_WIDTH])

    xp = list(n_zb)
    t_inv = [eye + n for n in n_zb]
    lru_parts = []
    span = 2
    while span < lc:
        for h in heads:
            xp[h] = _dot(xp[h], xp[h])
            t_inv[h] = t_inv[h] + _dot(t_inv[h], xp[h])
            if span == 2:
                lru_parts.append(lru_piece(h))
        span *= 2
    h_prev = hcar_s[0:1, :]
    h_groups = []
    for su, sa, gate_b in lru_parts:
        for gidx in range(piece // SUBLANES):
            gs = slice(gidx * SUBLANES, (gidx + 1) * SUBLANES)
            hg = su[gs] + sa[gs] * h_prev
            h_groups.append(hg * gate_b[gs])
            h_prev = hg[SUBLANES - 1:SUBLANES, :]
    hcar_s[...] = jnp.broadcast_to(h_prev, hcar_s.shape)
    y_b = jnp.concatenate(h_groups, axis=0)
    wv = [_dot(a_zk[h], v_h[h]) for h in heads]
    tzu = [_dot(t_inv[h], jnp.concatenate([z_h[h], wv[h]], axis=1)) for h in heads]
    ar = [_dot(a_rb[h], tzu[h]) for h in heads]
    m_y = [r_h[h] + ar[h][:, 0:hd] for h in heads]
    y1 = [ar[h][:, hd:2 * hd] + _dot(a_rk[h], v_h[h]) for h in heads]
    phi, gam = [], []
    for h in heads:
        b_blk = jnp.where(same, jnp.concatenate([bh_t[hsl[h], :]] * nchunk, axis=0), 0.0)
        k_blk = jnp.where(same, jnp.concatenate([kh_t[hsl[h], :]] * nchunk, axis=0), 0.0)
        pg = _dot(b_blk, tzu[h])
        phi.append(pg[:, 0:hd])
        gam.append(pg[:, hd:2 * hd] + _dot(k_blk, v_h[h]))
    st = [state_s[h] for h in heads]
    y_rows = [[] for _ in heads]
    for c in range(nchunk):
        cs = slice(c * lc, (c + 1) * lc)
        for h in heads:
            y_rows[h].append(_dot(m_y[h][cs], st[h]) + y1[h][cs])
            decay = pt_t[hsl[h], c * lc:c * lc + 1]
            st[h] = st[h] * decay + _dot(phi[h][cs], st[h]) + gam[h][cs]
    for h in heads:
        state_s[h] = st[h]
    y = jnp.concatenate([jnp.concatenate(rows, axis=0) for rows in y_rows], axis=1)

    mean = _half_dot(y, hones_ref) * (1.0 / hd)
    yc = y - mean
    var = _half_dot(yc * yc, hones_ref) * (1.0 / hd)
    y = yc * lax.rsqrt(var + GN_EPS) * lnw_ref[...] + lnb_ref[...]
    y_a = (y + bonus) * g

    y_cat = jnp.concatenate([y_a, y_b], axis=1).astype(BF16)
    o_ref[...] = x + jnp.dot(y_cat, wout_ref[...], preferred_element_type=F32)


def _block_diag(w):
    h, n, _ = w.shape
    eye = jnp.eye(h, dtype=w.dtype)
    return (eye[:, None, :, None] * w[:, :, None, :]).reshape(h * n, h * n)


def _row(v):
    return v.reshape(1, -1).astype(F32)


def _const_spec(shape):
    nd = len(shape)
    return pl.BlockSpec(shape, lambda *_: (0,) * nd, pipeline_mode=pl.Buffered(1))


def _even_layer(x2, bsz, seq, norm_g, w_in, mu, w0, w2, a0, a2, g2, k_k, k_a, r_k, ln_w, ln_b,
                conv_w, conv_b, gate_a_w, gate_a_b, gate_x_w, gate_x_b, lru_lambda, w_out):
    tm = TM_MIX
    nt = seq // tm
    rw = RWKV_WIDTH
    wwa = jnp.zeros((W_LORA + A_LORA, 2 * rw), F32)
    wwa = wwa.at[:W_LORA, :rw].set(w2).at[W_LORA:, rw:].set(a2).astype(BF16)
    def halves(w):
        hh = w.shape[0] // 2
        return jnp.stack([_block_diag(w[:hh]), _block_diag(w[hh:])]).astype(BF16)

    hones = halves(jnp.ones((RWKV_HEADS, RWKV_HEAD, RWKV_HEAD), F32))
    consts = [
        _row(norm_g), w_in.astype(BF16), _row(mu), _row(w0), wwa, _row(a0), g2.astype(BF16),
        _row(k_k), _row(k_a), _row(r_k), _row(ln_w), _row(ln_b), conv_w.astype(F32), _row(conv_b),
        halves(gate_a_w), _row(gate_a_b), halves(gate_x_w),
        _row(gate_x_b), _row(lru_lambda), hones, w_out.astype(BF16),
    ]
    in_specs = [pl.BlockSpec((tm, D_MODEL), lambda b, i: (b * nt + i, 0))]
    in_specs += [_const_spec(c.shape) for c in consts]
    return pl.pallas_call(
        functools.partial(_even_body, tm=tm),
        grid=(bsz, nt),
        in_specs=in_specs,
        out_specs=pl.BlockSpec((tm, D_MODEL), lambda b, i: (b * nt + i, 0)),
        out_shape=jax.ShapeDtypeStruct(x2.shape, F32),
        scratch_shapes=[
            pltpu.VMEM((SUBLANES, EVEN_IN_COLS), F32),
            pltpu.VMEM((RWKV_HEADS, RWKV_HEAD, RWKV_HEAD), F32),
            pltpu.VMEM((SUBLANES, LRU_WIDTH), F32),
        ],
        compiler_params=pltpu.CompilerParams(
            dimension_semantics=("arbitrary", "arbitrary"), vmem_limit_bytes=VMEM_LIMIT),
        name="even_mixer",
    )(x2, *consts)


def _s5_param_body(are_ref, aim_ref, ldt_ref, bre_ref, bim_ref, abre_ref, abim_ref, bbre_ref, bbim_ref):
    lam_re = jnp.minimum(are_ref[...], -1e-4)
    lam_im = aim_ref[...]
    dt = jnp.exp(ldt_ref[...])
    mag = jnp.exp(lam_re * dt)
    ab_re = mag * jnp.cos(lam_im * dt)
    ab_im = mag * jnp.sin(lam_im * dt)
    den = lam_re * lam_re + lam_im * lam_im
    zr = ab_re - 1.0
    q_re = (zr * lam_re + ab_im * lam_im) / den
    q_im = (ab_im * lam_re - zr * lam_im) / den
    abre_ref[...] = ab_re
    abim_ref[...] = ab_im
    for c in range(S5_GROUP):
        bbre_ref[c] = q_re * bre_ref[c] - q_im * bim_ref[c]
        bbim_ref[c] = q_re * bim_ref[c] + q_im * bre_ref[c]


def _s5_params(a_re, a_im, log_dt, b_re, b_im):
    g, n = S5_GROUPS, S5_STATE
    shp = jax.ShapeDtypeStruct
    return pl.pallas_call(
        _s5_param_body,
        out_shape=(shp((g, n), F32), shp((g, n), F32),
                   shp((S5_GROUP, g, n), F32), shp((S5_GROUP, g, n), F32)),
        name="s5_params",
    )(a_re.astype(F32), a_im.astype(F32), log_dt.reshape(g, 1).astype(F32),
      jnp.transpose(b_re, (2, 0, 1)).astype(F32), jnp.transpose(b_im, (2, 0, 1)).astype(F32))


def _odd_body(x_ref, perm_ref, permt_ref, ng_ref, win_ref, bre_ref, bim_ref, cre_ref, cim_ref,
              are_ref, aim_ref, d_ref, wglu_ref, o_ref, sre_s, sim_s, car_s, *, tt):
    i = pl.program_id(1)
    rows = SUBLANES * tt
    nstrip = sre_s.shape[0]
    nblk = bre_ref.shape[0]
    per = nstrip // nblk
    cw = bre_ref.shape[1]

    @pl.when(i == 0)
    def _():
        car_s[...] = jnp.zeros_like(car_s)

    x = x_ref[...].reshape(rows, D_MODEL)
    xn = _rms(x, ng_ref[...]).astype(BF16)
    xn = jnp.dot(perm_ref[...], xn, preferred_element_type=F32).astype(BF16)
    u = jnp.dot(xn, win_ref[...], preferred_element_type=F32)
    ub = u.astype(BF16)
    for j in range(nblk):
        uj = ub[:, j * cw:(j + 1) * cw]
        pr = jnp.dot(uj, bre_ref[j], preferred_element_type=F32)
        pi = jnp.dot(uj, bim_ref[j], preferred_element_type=F32)
        for q in range(per):
            sre_s[j * per + q] = pr[:, q * S5_LANES:(q + 1) * S5_LANES]
            sim_s[j * per + q] = pi[:, q * S5_LANES:(q + 1) * S5_LANES]

    for s in range(nstrip):
        ar = are_ref[s]
        ai = aim_ref[s]
        xr = car_s[0, s]
        xi = car_s[1, s]
        for t in range(tt):
            rs = slice(t * SUBLANES, (t + 1) * SUBLANES)
            xr, xi = (ar * xr - ai * xi + sre_s[s, rs, :], ar * xi + ai * xr + sim_s[s, rs, :])
            sre_s[s, rs, :] = xr
            sim_s[s, rs, :] = xi
        car_s[0, s] = xr
        car_s[1, s] = xi

    ys = []
    for j in range(nblk):
        acc = None
        for q in range(per):
            rsl = slice(q * S5_LANES, (q + 1) * S5_LANES)
            t = (_dot(sre_s[j * per + q], cre_ref[j, rsl, :])
                 - _dot(sim_s[j * per + q], cim_ref[j, rsl, :]))
            acc = t if acc is None else acc + t
        ys.append(acc)
    y = jnp.concatenate(ys, axis=1)
    yy = jax.nn.gelu(y + d_ref[...] * u).astype(BF16)
    yy = jnp.dot(permt_ref[...], yy, preferred_element_type=F32).astype(BF16)
    og = jnp.dot(yy, wglu_ref[...], preferred_element_type=F32)
    out = x + og[:, 0:D_MODEL] * jax.nn.sigmoid(og[:, D_MODEL:2 * D_MODEL])
    o_ref[...] = out.reshape(o_ref.shape)


def _odd_layer(x2, bsz, seq, norm_g, w_in, a_re, a_im, log_dt, b_re, b_im, c_re, c_im, d_skip, w_glu):
    tt = TT_ODD
    assert bsz % SUBLANES == 0 and seq % tt == 0
    nt = seq // tt
    rows = SUBLANES * tt
    g, n, gc = S5_GROUPS, S5_STATE, S5_GROUP
    gpb = 8
    nblk = g // gpb
    nstrip = g * n // S5_LANES
    ab_re, ab_im, bb_re, bb_im = _s5_params(a_re, a_im, log_dt, b_re, b_im)
    eye = jnp.eye(gpb, dtype=F32)

    def b_blocks(bb):
        t = bb.reshape(gc, nblk, gpb, n)
        t = jnp.einsum('cjgn,gh->jgchn', t, eye)
        return t.reshape(nblk, gpb * gc, gpb * n).astype(BF16)

    def c_blocks(cc):
        t = cc.astype(F32).reshape(nblk, gpb, gc, n)
        t = jnp.einsum('jgcn,gh->jgnhc', t, eye)
        return t.reshape(nblk, gpb * n, gpb * gc).astype(BF16)

    def strips(p):
        return jnp.broadcast_to(p.reshape(nstrip, 1, S5_LANES), (nstrip, SUBLANES, S5_LANES))

    dst = jnp.arange(rows)
    src = (dst % SUBLANES) * tt + dst // SUBLANES
    perm = (src[:, None] == jnp.arange(rows)[None, :]).astype(BF16)

    consts = [perm, perm.T, _row(norm_g), w_in.astype(BF16), b_blocks(bb_re), b_blocks(bb_im),
              c_blocks(c_re), c_blocks(c_im), strips(ab_re), strips(ab_im), _row(d_skip),
              w_glu.astype(BF16)]
    x4 = x2.reshape(bsz, nt, tt, D_MODEL)
    xspec = pl.BlockSpec((SUBLANES, 1, tt, D_MODEL), lambda b, i: (b, i, 0, 0))
    out = pl.pallas_call(
        functools.partial(_odd_body, tt=tt),
        grid=(bsz // SUBLANES, nt),
        in_specs=[xspec] + [_const_spec(c.shape) for c in consts],
        out_specs=xspec,
        out_shape=jax.ShapeDtypeStruct(x4.shape, F32),
        scratch_shapes=[
            pltpu.VMEM((nstrip, rows, S5_LANES), F32),
            pltpu.VMEM((nstrip, rows, S5_LANES), F32),
            pltpu.VMEM((2, nstrip, SUBLANES, S5_LANES), F32),
        ],
        compiler_params=pltpu.CompilerParams(
            dimension_semantics=("arbitrary", "arbitrary"), vmem_limit_bytes=VMEM_LIMIT),
        name="odd_mixer",
    )(x4, *consts)
    return out.reshape(x2.shape)


def _ffn_body(x_ref, ng_ref, wup_ref, cw_ref, cb_ref, wd_ref, fg_ref, o_ref,
              h_s, act_s, *, tm, tiles_per_seq, final_norm):
    i = pl.program_id(0)
    tf = TF_FFN
    x = x_ref[...]
    xn = _rms(x, ng_ref[...]).astype(BF16)

    @pl.when(i % tiles_per_seq == 0)
    def _():
        h_s[...] = jnp.zeros_like(h_s)

    def conv(c0):
        h = jnp.dot(xn, wup_ref[:, c0:c0 + tf], preferred_element_type=F32)
        full = jnp.concatenate([h_s[:, c0:c0 + tf], h], axis=0)
        h_s[:, c0:c0 + tf] = h[tm - SUBLANES:tm, :]
        out = cb_ref[:, c0:c0 + tf] + cw_ref[FFN_CONV - 1:FFN_CONV, c0:c0 + tf] * h
        for t in range(FFN_CONV - 1):
            lag = FFN_CONV - 1 - t
            out = out + cw_ref[t:t + 1, c0:c0 + tf] * full[SUBLANES - lag:SUBLANES - lag + tm, :]
        return out

    for c in range(D_FF // tf):
        act = jax.nn.silu(conv(c * tf)) * conv(D_FF + c * tf)
        act_s[:, c * tf:(c + 1) * tf] = act.astype(BF16)
    y = x + jnp.dot(act_s[...], wd_ref[...], preferred_element_type=F32)
    if final_norm:
        y = _rms(y, fg_ref[...])
    o_ref[...] = y


def _ffn_layer(x2, seq, norm_g, w_up, conv_w, conv_b, w_down, final_g, final_norm):
    tm = TM_FFN
    m = x2.shape[0]
    consts = [_row(norm_g), w_up.astype(BF16), conv_w.astype(F32), _row(conv_b),
              w_down.astype(BF16), _row(final_g)]
    in_specs = [pl.BlockSpec((tm, D_MODEL), lambda i: (i, 0))]
    in_specs += [_const_spec(c.shape) for c in consts]
    return pl.pallas_call(
        functools.partial(_ffn_body, tm=tm, tiles_per_seq=seq // tm, final_norm=final_norm),
        grid=(m // tm,),
        in_specs=in_specs,
        out_specs=pl.BlockSpec((tm, D_MODEL), lambda i: (i, 0)),
        out_shape=jax.ShapeDtypeStruct(x2.shape, F32),
        scratch_shapes=[
            pltpu.VMEM((SUBLANES, 2 * D_FF), F32),
            pltpu.VMEM((tm, D_FF), BF16),
        ],
        compiler_params=pltpu.CompilerParams(
            dimension_semantics=("arbitrary",), vmem_limit_bytes=VMEM_LIMIT),
        name="conv_ffn",
    )(x2, *consts)


def kernel(x, e_norm_g, e_w_in, e_mu, e_w0, e_w2, e_a0, e_a2, e_g2, e_k_k, e_k_a, e_r_k,
           e_ln_w, e_ln_b, e_conv_w, e_conv_b, e_gate_a_w, e_gate_a_b, e_gate_x_w, e_gate_x_b,
           e_lru_lambda, e_w_out, o_norm_g, o_w_in, o_A_re, o_A_im, o_log_dt, o_B_re, o_B_im,
           o_C_re, o_C_im, o_D, o_w_glu, f_norm_g, f_w_up, f_conv_w, f_conv_b, f_w_down,
           final_norm_g):
    bsz, seq, dm = x.shape
    assert dm == D_MODEL and seq % TM_FFN == 0 and seq % TM_MIX == 0
    depth = f_norm_g.shape[0]
    h = x.reshape(bsz * seq, dm).astype(F32)
    for i in range(depth):
        j = i // 2
        if i % 2 == 0:
            h = _even_layer(h, bsz, seq, e_norm_g[j], e_w_in[j], e_mu[j], e_w0[j], e_w2[j],
                            e_a0[j], e_a2[j], e_g2[j], e_k_k[j], e_k_a[j], e_r_k[j], e_ln_w[j],
                            e_ln_b[j], e_conv_w[j], e_conv_b[j], e_gate_a_w[j], e_gate_a_b[j],
                            e_gate_x_w[j], e_gate_x_b[j], e_lru_lambda[j], e_w_out[j])
        else:
            h = _odd_layer(h, bsz, seq, o_norm_g[j], o_w_in[j], o_A_re[j], o_A_im[j],
                           o_log_dt[j], o_B_re[j], o_B_im[j], o_C_re[j], o_C_im[j], o_D[j],
                           o_w_glu[j])
        h = _ffn_layer(h, seq, f_norm_g[i], f_w_up[i], f_conv_w[i], f_conv_b[i], f_w_down[i],
                       final_norm_g, final_norm=(i == depth - 1))
    return h.reshape(bsz, seq, dm).astype(x.dtype)
```

```python
import functools

import jax
import jax.numpy as jnp
from jax import lax
from jax.experimental import pallas as pl
from jax.experimental.pallas import tpu as pltpu

F32 = jnp.float32
BF16 = jnp.bfloat16

D_MODEL = 1024
RWKV_WIDTH = 512
RWKV_HEAD = 64
RWKV_HEADS = 8
W_LORA = 64
A_LORA = 64
G_LORA = 128
LRU_WIDTH = 512
LRU_BLOCKS = 8
LRU_BLOCK = 64
LRU_CONV = 4
LRU_C = 8.0
S5_WIDTH = 1024
S5_GROUP = 16
S5_GROUPS = 64
S5_STATE = 64
D_FF = 2816
FFN_CONV = 3
NORM_EPS = 1e-6
GN_EPS = 64e-5
RWKV_SHIFT_COLS = 3 * RWKV_WIDTH + W_LORA + A_LORA + G_LORA
EVEN_IN_COLS = RWKV_SHIFT_COLS + 2 * LRU_WIDTH

SUBLANES = 8
TM_MIX = 256
WKV_CHUNK = 64
ROW_BLOCK = 32
TT_ODD = 32
S5_LANES = 256
TM_FFN = 512
TF_FFN = 256
VMEM_LIMIT = 56 * 1024 * 1024


def _dot(a, b):
    return jnp.dot(a.astype(BF16), b.astype(BF16), preferred_element_type=F32)


def _dot_nt(a, b):
    return lax.dot_general(a.astype(BF16), b.astype(BF16), (((1,), (1,)), ((), ())),
                           preferred_element_type=F32)


def _split(x, n):
    parts = []
    for _ in range(n - 1):
        h = x.astype(BF16)
        parts.append(h)
        x = x - h.astype(F32)
    parts.append(x.astype(BF16))
    return parts


def _dot_split_rhs(a_bf16, b, n):
    acc = None
    for piece in _split(b, n):
        t = jnp.dot(a_bf16, piece, preferred_element_type=F32)
        acc = t if acc is None else acc + t
    return acc


def _rms(x, g):
    ms = jnp.mean(x * x, axis=-1, keepdims=True)
    return x * lax.rsqrt(ms + NORM_EPS) * g


def _half_dot(x, w_ref):
    half = x.shape[1] // 2
    return jnp.concatenate([_dot(x[:, 0:half], w_ref[0]), _dot(x[:, half:2 * half], w_ref[1])], axis=1)


def _even_body(x_ref, ng_ref, win_ref, mu_ref, w0_ref, wwa_ref, a0_ref, g2_ref,
               kk_ref, ka_ref, rk_ref, lnw_ref, lnb_ref, cw_ref, cb_ref, gaw_ref, gab_ref,
               gxw_ref, gxb_ref, lam_ref, hones_ref, wout_ref, o_ref,
               p_s, state_s, hcar_s, *, tm):
    i = pl.program_id(1)
    rw = RWKV_WIDTH
    hd = RWKV_HEAD
    lc = WKV_CHUNK
    nchunk = tm // lc

    @pl.when(i == 0)
    def _():
        state_s[...] = jnp.zeros_like(state_s)
        hcar_s[...] = jnp.zeros_like(hcar_s)
        p_s[...] = jnp.zeros_like(p_s)

    x = x_ref[...]
    xn = _rms(x, ng_ref[...]).astype(BF16)
    p = jnp.dot(xn, win_ref[...], preferred_element_type=F32)
    p_full = jnp.concatenate([p_s[...], p], axis=0)
    p_s[...] = p[tm - SUBLANES:tm, :]

    def lagged(c_lo, c_hi, lag):
        return p_full[SUBLANES - lag:SUBLANES - lag + tm, c_lo:c_hi]

    nblk = tm // ROW_BLOCK
    blocks = [slice(b * ROW_BLOCK, (b + 1) * ROW_BLOCK) for b in range(nblk)]
    cat = lambda parts: jnp.concatenate(parts, axis=0)
    lane = lax.broadcasted_iota(jnp.int32, (ROW_BLOCK, W_LORA + A_LORA), 1)
    r_b, k_b, v_b, wa_b, gd_b = [], [], [], [], []
    for rs in blocks:
        pa = p[rs, 0:RWKV_SHIFT_COLS]
        pp = p_full[SUBLANES - 1 + rs.start:SUBLANES - 1 + rs.stop, 0:RWKV_SHIFT_COLS]
        pm = pa + mu_ref[...] * (pp - pa)
        r_b.append(pm[:, 0:rw])
        k_b.append(pm[:, rw:2 * rw])
        v_b.append(pm[:, 2 * rw:3 * rw])
        wa = pm[:, 3 * rw:3 * rw + W_LORA + A_LORA]
        wa_b.append(jnp.where(lane < W_LORA, jnp.tanh(wa), wa).astype(BF16))
        gd_b.append(jax.nn.sigmoid(pm[:, 3 * rw + W_LORA + A_LORA:RWKV_SHIFT_COLS]).astype(BF16))
    lora = _dot(cat(wa_b), wwa_ref[...])
    g = _dot(cat(gd_b), g2_ref[...])
    logw_b, a_b, kk_b, sq_b = [], [], [], []
    for b, rs in enumerate(blocks):
        w_log = -jax.nn.softplus(-(w0_ref[...] + lora[rs, 0:rw])) - 0.5
        logw_b.append(-jnp.exp(w_log))
        a_b.append(jax.nn.sigmoid(a0_ref[...] + lora[rs, rw:2 * rw]))
        kk = k_b[b] * kk_ref[...]
        kk_b.append(kk)
        sq_b.append((kk * kk).astype(BF16))
    logw = cat(logw_b)
    kk_ss = _half_dot(cat(sq_b), hones_ref)
    row = lax.broadcasted_iota(jnp.int32, (tm, tm), 0)
    col = lax.broadcasted_iota(jnp.int32, (tm, tm), 1)
    same = (row // lc) == (col // lc)
    incl = same & (row >= col)
    strict = same & (row > col)
    cum = _dot_split_rhs(incl.astype(BF16), logw, 3)
    bon_b, rt_b, zt_b, kt_b, bt_b, kh_b, bh_b, pt_b = [], [], [], [], [], [], [], []
    for b, rs in enumerate(blocks):
        c_end = (rs.start // lc + 1) * lc
        cm = cum[rs]
        tot = cum[c_end - 1:c_end, :]
        a = a_b[b]
        kk = kk_b[b] * jnp.minimum(lax.rsqrt(kk_ss[rs]), 1e12)
        k2 = k_b[b] * (1.0 + (a - 1.0) * ka_ref[...])
        bon_b.append((r_b[b] * k2 * rk_ref[...]).astype(BF16))
        bv = kk * a
        inv_p = jnp.exp(-cm)
        to_end = jnp.exp(tot - cm)
        rt_b.append(r_b[b] * jnp.exp(cm))
        zt_b.append(-kk * jnp.exp(cm - logw_b[b]))
        kt_b.append(k2 * inv_p)
        bt_b.append(bv * inv_p)
        kh_b.append(k2 * to_end)
        bh_b.append(bv * to_end)
        pt_b.append(jnp.broadcast_to(jnp.exp(tot), (ROW_BLOCK, rw)))
    v = cat(v_b)
    bonus = _half_dot(cat(bon_b), hones_ref) * v
    rt, zt, kt, bt = cat(rt_b), cat(zt_b), cat(kt_b), cat(bt_b)
    kh_t = cat(kh_b).T
    bh_t = cat(bh_b).T
    pt_t = cat(pt_b).T
    eye = (row == col).astype(F32)

    heads = range(RWKV_HEADS)
    hsl = [slice(h * hd, (h + 1) * hd) for h in heads]
    z_h = [zt[:, s] for s in hsl]
    r_h = [rt[:, s] for s in hsl]
    v_h = [v[:, s] for s in hsl]
    n_zb, a_zk, a_rb, a_rk = [], [], [], []
    for h in heads:
        zr = jnp.concatenate([z_h[h], r_h[h]], axis=0)
        bk = jnp.concatenate([bt[:, hsl[h]], kt[:, hsl[h]]], axis=0)
        aa = _dot_nt(zr, bk)
        n_zb.append(jnp.where(strict, aa[0:tm, 0:tm], 0.0))
        a_zk.append(jnp.where(strict, aa[0:tm, tm:2 * tm], 0.0))
        a_rb.append(jnp.where(incl, aa[tm:2 * tm, 0:tm], 0.0))
        a_rk.append(jnp.where(incl, aa[tm:2 * tm, tm:2 * tm], 0.0))
    c0 = RWKV_SHIFT_COLS
    xc = cb_ref[...]
    for j in range(LRU_CONV):
        xc = xc + cw_ref[j:j + 1, :] * lagged(c0, c0 + LRU_WIDTH, LRU_CONV - 1 - j)
    gate_a = _half_dot(xc, gaw_ref) + gab_ref[...]
    gate_x = _half_dot(xc, gxw_ref) + gxb_ref[...]
    neg_c_softplus = -LRU_C * jax.nn.softplus(-lam_ref[...])
    piece = tm // RWKV_HEADS
    in_group = lax.broadcasted_iota(jnp.int32, (piece, LRU_WIDTH), 0) % SUBLANES

    def lru_piece(q):
        rs = slice(q * piece, (q + 1) * piece)
        log_a = neg_c_softplus * jax.nn.sigmoid(gate_a[rs])
        sa = jnp.exp(log_a)
        su = xc[rs] * jax.nn.sigmoid(gate_x[rs]) * jnp.sqrt(1.0 - jnp.exp(2.0 * log_a))
        for d in (1, 2, 4):
            keep = in_group >= d
            su, sa = (su + sa * jnp.where(keep, pltpu.roll(su, d, 0), 0.0),
                      sa * jnp.where(keep, pltpu.roll(sa, d, 0), 1.0))
        return su, sa, jax.nn.gelu(p[rs, c0 + LRU_WIDTH:c0 + 2 * LRU_WIDTH])

    def side_by_side(m):
        out = m[0:lc]
        for c in range(1, nchunk):
            out = out + m[c * lc:(c + 1) * lc]
        return out

    def block_diag(m):
        return jnp.where(same, jnp.concatenate([m] * nchunk, axis=0), 0.0)

    xs = [side_by_side(n) for n in n_zb]
    ts = [side_by_side(eye) + x for x in xs]
    lru_parts = []
    for h in heads:
        xs[h] = _dot(xs[h], block_diag(xs[h]))
        lru_parts.append(lru_piece(h))
    span = 4
    while span < lc:
        for h in heads:
            both = _dot(jnp.concatenate([ts[h], xs[h]], axis=0), block_diag(xs[h]))
            ts[h] = ts[h] + both[0:lc]
            xs[h] = both[lc:2 * lc]
        span *= 2
    t_inv = [block_diag(ts[h] + _dot(ts[h], block_diag(xs[h]))) for h in heads]
    h_prev = hcar_s[0:1, :]
    h_groups = []
    for su, sa, gate_b in lru_parts:
        for gidx in range(piece // SUBLANES):
            gs = slice(gidx * SUBLANES, (gidx + 1) * SUBLANES)
            hg = su[gs] + sa[gs] * h_prev
            h_groups.append(hg * gate_b[gs])
            h_prev = hg[SUBLANES - 1:SUBLANES, :]
    hcar_s[...] = jnp.broadcast_to(h_prev, hcar_s.shape)
    y_b = jnp.concatenate(h_groups, axis=0)
    wv = [_dot(a_zk[h], v_h[h]) for h in heads]
    tzu = [_dot(t_inv[h], jnp.concatenate([z_h[h], wv[h]], axis=1)) for h in heads]
    ar = [_dot(a_rb[h], tzu[h]) for h in heads]
    m_y = [r_h[h] + ar[h][:, 0:hd] for h in heads]
    y1 = [ar[h][:, hd:2 * hd] + _dot(a_rk[h], v_h[h]) for h in heads]
    phi, gam = [], []
    for h in heads:
        b_blk = jnp.where(same, jnp.concatenate([bh_t[hsl[h], :]] * nchunk, axis=0), 0.0)
        k_blk = jnp.where(same, jnp.concatenate([kh_t[hsl[h], :]] * nchunk, axis=0), 0.0)
        pg = _dot(b_blk, tzu[h])
        phi.append(pg[:, 0:hd])
        gam.append(pg[:, hd:2 * hd] + _dot(k_blk, v_h[h]))
    st = [state_s[h] for h in heads]
    y_rows = [[] for _ in heads]
    for c in range(nchunk):
        cs = slice(c * lc, (c + 1) * lc)
        for h in heads:
            y_rows[h].append(_dot(m_y[h][cs], st[h]) + y1[h][cs])
            decay = pt_t[hsl[h], c * lc:c * lc + 1]
            st[h] = st[h] * decay + _dot(phi[h][cs], st[h]) + gam[h][cs]
    for h in heads:
        state_s[h] = st[h]
    y = jnp.concatenate([jnp.concatenate(rows, axis=0) for rows in y_rows], axis=1)

    mean = _half_dot(y, hones_ref) * (1.0 / hd)
    yc = y - mean
    var = _half_dot(yc * yc, hones_ref) * (1.0 / hd)
    y = yc * lax.rsqrt(var + GN_EPS) * lnw_ref[...] + lnb_ref[...]
    y_a = (y + bonus) * g

    y_cat = jnp.concatenate([y_a, y_b], axis=1).astype(BF16)
    o_ref[...] = x + jnp.dot(y_cat, wout_ref[...], preferred_element_type=F32)


def _block_diag(w):
    h, n, _ = w.shape
    eye = jnp.eye(h, dtype=w.dtype)
    return (eye[:, None, :, None] * w[:, :, None, :]).reshape(h * n, h * n)


def _row(v):
    return v.reshape(1, -1).astype(F32)


def _const_spec(shape):
    nd = len(shape)
    return pl.BlockSpec(shape, lambda *_: (0,) * nd, pipeline_mode=pl.Buffered(1))


def _even_layer(x2, bsz, seq, norm_g, w_in, mu, w0, w2, a0, a2, g2, k_k, k_a, r_k, ln_w, ln_b,
                conv_w, conv_b, gate_a_w, gate_a_b, gate_x_w, gate_x_b, lru_lambda, w_out):
    tm = TM_MIX
    nt = seq // tm
    rw = RWKV_WIDTH
    wwa = jnp.zeros((W_LORA + A_LORA, 2 * rw), F32)
    wwa = wwa.at[:W_LORA, :rw].set(w2).at[W_LORA:, rw:].set(a2).astype(BF16)
    def halves(w):
        hh = w.shape[0] // 2
        return jnp.stack([_block_diag(w[:hh]), _block_diag(w[hh:])]).astype(BF16)

    hones = halves(jnp.ones((RWKV_HEADS, RWKV_HEAD, RWKV_HEAD), F32))
    consts = [
        _row(norm_g), w_in.astype(BF16), _row(mu), _row(w0), wwa, _row(a0), g2.astype(BF16),
        _row(k_k), _row(k_a), _row(r_k), _row(ln_w), _row(ln_b), conv_w.astype(F32), _row(conv_b),
        halves(gate_a_w), _row(gate_a_b), halves(gate_x_w),
        _row(gate_x_b), _row(lru_lambda), hones, w_out.astype(BF16),
    ]
    in_specs = [pl.BlockSpec((tm, D_MODEL), lambda b, i: (b * nt + i, 0))]
    in_specs += [_const_spec(c.shape) for c in consts]
    return pl.pallas_call(
        functools.partial(_even_body, tm=tm),
        grid=(bsz, nt),
        in_specs=in_specs,
        out_specs=pl.BlockSpec((tm, D_MODEL), lambda b, i: (b * nt + i, 0)),
        out_shape=jax.ShapeDtypeStruct(x2.shape, F32),
        scratch_shapes=[
            pltpu.VMEM((SUBLANES, EVEN_IN_COLS), F32),
            pltpu.VMEM((RWKV_HEADS, RWKV_HEAD, RWKV_HEAD), F32),
            pltpu.VMEM((SUBLANES, LRU_WIDTH), F32),
        ],
        compiler_params=pltpu.CompilerParams(
            dimension_semantics=("arbitrary", "arbitrary"), vmem_limit_bytes=VMEM_LIMIT),
        name="even_mixer",
    )(x2, *consts)


def _s5_param_body(are_ref, aim_ref, ldt_ref, bre_ref, bim_ref, abre_ref, abim_ref, bbre_ref, bbim_ref):
    lam_re = jnp.minimum(are_ref[...], -1e-4)
    lam_im = aim_ref[...]
    dt = jnp.exp(ldt_ref[...])
    mag = jnp.exp(lam_re * dt)
    ab_re = mag * jnp.cos(lam_im * dt)
    ab_im = mag * jnp.sin(lam_im * dt)
    den = lam_re * lam_re + lam_im * lam_im
    zr = ab_re - 1.0
    q_re = (zr * lam_re + ab_im * lam_im) / den
    q_im = (ab_im * lam_re - zr * lam_im) / den
    abre_ref[...] = ab_re
    abim_ref[...] = ab_im
    for c in range(S5_GROUP):
        bbre_ref[c] = q_re * bre_ref[c] - q_im * bim_ref[c]
        bbim_ref[c] = q_re * bim_ref[c] + q_im * bre_ref[c]


def _s5_params(a_re, a_im, log_dt, b_re, b_im):
    g, n = S5_GROUPS, S5_STATE
    shp = jax.ShapeDtypeStruct
    return pl.pallas_call(
        _s5_param_body,
        out_shape=(shp((g, n), F32), shp((g, n), F32),
                   shp((S5_GROUP, g, n), F32), shp((S5_GROUP, g, n), F32)),
        name="s5_params",
    )(a_re.astype(F32), a_im.astype(F32), log_dt.reshape(g, 1).astype(F32),
      jnp.transpose(b_re, (2, 0, 1)).astype(F32), jnp.transpose(b_im, (2, 0, 1)).astype(F32))


def _odd_body(x_ref, perm_ref, permt_ref, ng_ref, win_ref, bre_ref, bim_ref, cre_ref, cim_ref,
              are_ref, aim_ref, d_ref, wglu_ref, o_ref, sre_s, sim_s, car_s, *, tt):
    i = pl.program_id(1)
    rows = SUBLANES * tt
    nstrip = sre_s.shape[0]
    nblk = bre_ref.shape[0]
    per = nstrip // nblk
    cw = bre_ref.shape[1]

    @pl.when(i == 0)
    def _():
        car_s[...] = jnp.zeros_like(car_s)

    x = x_ref[...].reshape(rows, D_MODEL)
    xn = _rms(x, ng_ref[...]).astype(BF16)
    xn = jnp.dot(perm_ref[...], xn, preferred_element_type=F32).astype(BF16)
    u = jnp.dot(xn, win_ref[...], preferred_element_type=F32)
    ub = u.astype(BF16)
    for j in range(nblk):
        uj = ub[:, j * cw:(j + 1) * cw]
        pr = jnp.dot(uj, bre_ref[j], preferred_element_type=F32)
        pi = jnp.dot(uj, bim_ref[j], preferred_element_type=F32)
        for q in range(per):
            sre_s[j * per + q] = pr[:, q * S5_LANES:(q + 1) * S5_LANES]
            sim_s[j * per + q] = pi[:, q * S5_LANES:(q + 1) * S5_LANES]

    for s in range(nstrip):
        ar = are_ref[s]
        ai = aim_ref[s]
        xr = car_s[0, s]
        xi = car_s[1, s]
        for t in range(tt):
            rs = slice(t * SUBLANES, (t + 1) * SUBLANES)
            xr, xi = (ar * xr - ai * xi + sre_s[s, rs, :], ar * xi + ai * xr + sim_s[s, rs, :])
            sre_s[s, rs, :] = xr
            sim_s[s, rs, :] = xi
        car_s[0, s] = xr
        car_s[1, s] = xi

    ys = []
    for j in range(nblk):
        acc = None
        for q in range(per):
            rsl = slice(q * S5_LANES, (q + 1) * S5_LANES)
            t = (_dot(sre_s[j * per + q], cre_ref[j, rsl, :])
                 - _dot(sim_s[j * per + q], cim_ref[j, rsl, :]))
            acc = t if acc is None else acc + t
        ys.append(acc)
    y = jnp.concatenate(ys, axis=1)
    yy = jax.nn.gelu(y + d_ref[...] * u).astype(BF16)
    yy = jnp.dot(permt_ref[...], yy, preferred_element_type=F32).astype(BF16)
    og = jnp.dot(yy, wglu_ref[...], preferred_element_type=F32)
    out = x + og[:, 0:D_MODEL] * jax.nn.sigmoid(og[:, D_MODEL:2 * D_MODEL])
    o_ref[...] = out.reshape(o_ref.shape)


def _odd_layer(x2, bsz, seq, norm_g, w_in, a_re, a_im, log_dt, b_re, b_im, c_re, c_im, d_skip, w_glu):
    tt = TT_ODD
    assert bsz % SUBLANES == 0 and seq % tt == 0
    nt = seq // tt
    rows = SUBLANES * tt
    g, n, gc = S5_GROUPS, S5_STATE, S5_GROUP
    gpb = 8
    nblk = g // gpb
    nstrip = g * n // S5_LANES
    ab_re, ab_im, bb_re, bb_im = _s5_params(a_re, a_im, log_dt, b_re, b_im)
    eye = jnp.eye(gpb, dtype=F32)

    def b_blocks(bb):
        t = bb.reshape(gc, nblk, gpb, n)
        t = jnp.einsum('cjgn,gh->jgchn', t, eye)
        return t.reshape(nblk, gpb * gc, gpb * n).astype(BF16)

    def c_blocks(cc):
        t = cc.astype(F32).reshape(nblk, gpb, gc, n)
        t = jnp.einsum('jgcn,gh->jgnhc', t, eye)
        return t.reshape(nblk, gpb * n, gpb * gc).astype(BF16)

    def strips(p):
        return jnp.broadcast_to(p.reshape(nstrip, 1, S5_LANES), (nstrip, SUBLANES, S5_LANES))

    dst = jnp.arange(rows)
    src = (dst % SUBLANES) * tt + dst // SUBLANES
    perm = (src[:, None] == jnp.arange(rows)[None, :]).astype(BF16)

    consts = [perm, perm.T, _row(norm_g), w_in.astype(BF16), b_blocks(bb_re), b_blocks(bb_im),
              c_blocks(c_re), c_blocks(c_im), strips(ab_re), strips(ab_im), _row(d_skip),
              w_glu.astype(BF16)]
    x4 = x2.reshape(bsz, nt, tt, D_MODEL)
    xspec = pl.BlockSpec((SUBLANES, 1, tt, D_MODEL), lambda b, i: (b, i, 0, 0))
    out = pl.pallas_call(
        functools.partial(_odd_body, tt=tt),
        grid=(bsz // SUBLANES, nt),
        in_specs=[xspec] + [_const_spec(c.shape) for c in consts],
        out_specs=xspec,
        out_shape=jax.ShapeDtypeStruct(x4.shape, F32),
        scratch_shapes=[
            pltpu.VMEM((nstrip, rows, S5_LANES), F32),
            pltpu.VMEM((nstrip, rows, S5_LANES), F32),
            pltpu.VMEM((2, nstrip, SUBLANES, S5_LANES), F32),
        ],
        compiler_params=pltpu.CompilerParams(
            dimension_semantics=("arbitrary", "arbitrary"), vmem_limit_bytes=VMEM_LIMIT),
        name="odd_mixer",
    )(x4, *consts)
    return out.reshape(x2.shape)


def _ffn_body(x_ref, ng_ref, wup_ref, cw_ref, cb_ref, wd_ref, fg_ref, o_ref,
              h_s, act_s, *, tm, tiles_per_seq, final_norm):
    i = pl.program_id(0)
    tf = TF_FFN
    x = x_ref[...]
    xn = _rms(x, ng_ref[...]).astype(BF16)

    @pl.when(i % tiles_per_seq == 0)
    def _():
        h_s[...] = jnp.zeros_like(h_s)

    def conv(c0):
        h = jnp.dot(xn, wup_ref[:, c0:c0 + tf], preferred_element_type=F32)
        full = jnp.concatenate([h_s[:, c0:c0 + tf], h], axis=0)
        h_s[:, c0:c0 + tf] = h[tm - SUBLANES:tm, :]
        out = cb_ref[:, c0:c0 + tf] + cw_ref[FFN_CONV - 1:FFN_CONV, c0:c0 + tf] * h
        for t in range(FFN_CONV - 1):
            lag = FFN_CONV - 1 - t
            out = out + cw_ref[t:t + 1, c0:c0 + tf] * full[SUBLANES - lag:SUBLANES - lag + tm, :]
        return out

    for c in range(D_FF // tf):
        act = jax.nn.silu(conv(c * tf)) * conv(D_FF + c * tf)
        act_s[:, c * tf:(c + 1) * tf] = act.astype(BF16)
    y = x + jnp.dot(act_s[...], wd_ref[...], preferred_element_type=F32)
    if final_norm:
        y = _rms(y, fg_ref[...])
    o_ref[...] = y


def _layer_spec(stacked, layer):
    nd = stacked.ndim - 1
    return pl.BlockSpec((None,) + stacked.shape[1:], lambda *_: (layer,) + (0,) * nd,
                        pipeline_mode=pl.Buffered(1))


def _ffn_layer(x2, seq, layer, norm_g, w_up_all, conv_w, conv_b, w_down_all, final_g, final_norm):
    tm = TM_FFN
    m = x2.shape[0]
    consts = [_row(norm_g), w_up_all, conv_w.astype(F32), _row(conv_b), w_down_all, _row(final_g)]
    in_specs = [pl.BlockSpec((tm, D_MODEL), lambda i: (i, 0))]
    in_specs += [_layer_spec(c, layer) if c.ndim == 3 else _const_spec(c.shape) for c in consts]
    return pl.pallas_call(
        functools.partial(_ffn_body, tm=tm, tiles_per_seq=seq // tm, final_norm=final_norm),
        grid=(m // tm,),
        in_specs=in_specs,
        out_specs=pl.BlockSpec((tm, D_MODEL), lambda i: (i, 0)),
        out_shape=jax.ShapeDtypeStruct(x2.shape, F32),
        scratch_shapes=[
            pltpu.VMEM((SUBLANES, 2 * D_FF), F32),
            pltpu.VMEM((tm, D_FF), BF16),
        ],
        compiler_params=pltpu.CompilerParams(
            dimension_semantics=("arbitrary",), vmem_limit_bytes=VMEM_LIMIT),
        name="conv_ffn",
    )(x2, *consts)


def kernel(x, e_norm_g, e_w_in, e_mu, e_w0, e_w2, e_a0, e_a2, e_g2, e_k_k, e_k_a, e_r_k,
           e_ln_w, e_ln_b, e_conv_w, e_conv_b, e_gate_a_w, e_gate_a_b, e_gate_x_w, e_gate_x_b,
           e_lru_lambda, e_w_out, o_norm_g, o_w_in, o_A_re, o_A_im, o_log_dt, o_B_re, o_B_im,
           o_C_re, o_C_im, o_D, o_w_glu, f_norm_g, f_w_up, f_conv_w, f_conv_b, f_w_down,
           final_norm_g):
    bsz, seq, dm = x.shape
    assert dm == D_MODEL and seq % TM_FFN == 0 and seq % TM_MIX == 0
    depth = f_norm_g.shape[0]
    h = x.reshape(bsz * seq, dm).astype(F32)
    f_w_up_bf = f_w_up.astype(BF16)
    f_w_down_bf = f_w_down.astype(BF16)
    for i in range(depth):
        j = i // 2
        if i % 2 == 0:
            h = _even_layer(h, bsz, seq, e_norm_g[j], e_w_in[j], e_mu[j], e_w0[j], e_w2[j],
                            e_a0[j], e_a2[j], e_g2[j], e_k_k[j], e_k_a[j], e_r_k[j], e_ln_w[j],
                            e_ln_b[j], e_conv_w[j], e_conv_b[j], e_gate_a_w[j], e_gate_a_b[j],
                            e_gate_x_w[j], e_gate_x_b[j], e_lru_lambda[j], e_w_out[j])
        else:
            h = _odd_layer(h, bsz, seq, o_norm_g[j], o_w_in[j], o_A_re[j], o_A_im[j],
                           o_log_dt[j], o_B_re[j], o_B_im[j], o_C_re[j], o_C_im[j], o_D[j],
                           o_w_glu[j])
        h = _ffn_layer(h, seq, i, f_norm_g[i], f_w_up_bf, f_conv_w[i], f_conv_b[i], f_w_down_bf,
                       final_norm_g, final_norm=(i == depth - 1))
    return h.reshape(bsz, seq, dm).astype(x.dtype)
```

```python
import functools

import jax
import jax.numpy as jnp
from jax import lax
from jax.experimental import pallas as pl
from jax.experimental.pallas import tpu as pltpu

F32 = jnp.float32
BF16 = jnp.bfloat16

D_MODEL = 1024
RWKV_WIDTH = 512
RWKV_HEAD = 64
RWKV_HEADS = 8
W_LORA = 64
A_LORA = 64
G_LORA = 128
LRU_WIDTH = 512
LRU_BLOCKS = 8
LRU_BLOCK = 64
LRU_CONV = 4
LRU_C = 8.0
S5_WIDTH = 1024
S5_GROUP = 16
S5_GROUPS = 64
S5_STATE = 64
D_FF = 2816
FFN_CONV = 3
NORM_EPS = 1e-6
GN_EPS = 64e-5
RWKV_SHIFT_COLS = 3 * RWKV_WIDTH + W_LORA + A_LORA + G_LORA
EVEN_IN_COLS = RWKV_SHIFT_COLS + 2 * LRU_WIDTH

SUBLANES = 8
TM_MIX = 256
WKV_CHUNK = 64
ROW_BLOCK = 64
TT_ODD = 32
S5_LANES = 256
TM_FFN = 1024
TF_FFN = 256
VMEM_LIMIT = 56 * 1024 * 1024


def _dot(a, b):
    return jnp.dot(a.astype(BF16), b.astype(BF16), preferred_element_type=F32)


def _dot_nt(a, b):
    return lax.dot_general(a.astype(BF16), b.astype(BF16), (((1,), (1,)), ((), ())),
                           preferred_element_type=F32)


def _split(x, n):
    parts = []
    for _ in range(n - 1):
        h = x.astype(BF16)
        parts.append(h)
        x = x - h.astype(F32)
    parts.append(x.astype(BF16))
    return parts


def _dot_split_rhs(a_bf16, b, n):
    acc = None
    for piece in _split(b, n):
        t = jnp.dot(a_bf16, piece, preferred_element_type=F32)
        acc = t if acc is None else acc + t
    return acc


def _rms(x, g):
    ms = jnp.mean(x * x, axis=-1, keepdims=True)
    return x * lax.rsqrt(ms + NORM_EPS) * g


def _half_dot(x, w_ref):
    half = x.shape[1] // 2
    return jnp.concatenate([_dot(x[:, 0:half], w_ref[0]), _dot(x[:, half:2 * half], w_ref[1])], axis=1)


def _run(stream):
    while True:
        try:
            next(stream)
        except StopIteration as stop:
            return stop.value


def _run_together(main, side):
    results, done = [None, None], [False, False]
    streams = (main, side)
    while not all(done):
        for idx, stream in enumerate(streams):
            if not done[idx]:
                try:
                    next(stream)
                except StopIteration as stop:
                    results[idx], done[idx] = stop.value, True
    return results


def _chain(*streams):
    out = []
    for stream in streams:
        out.append((yield from stream))
    return out


def _even_body(x_ref, ng_ref, win_ref, mu_ref, w0_ref, wwa_ref, a0_ref, g2_ref,
               kk_ref, ka_ref, rk_ref, lnw_ref, lnb_ref, cw_ref, cb_ref, gaw_ref, gab_ref,
               gxw_ref, gxb_ref, lam_ref, hones_ref, wout_ref, o_ref,
               p_s, state_s, hcar_s, *, tm):
    i = pl.program_id(1)
    rw = RWKV_WIDTH
    hw = rw
    hd = RWKV_HEAD
    hph = RWKV_HEADS
    lc = WKV_CHUNK
    nchunk = tm // lc

    @pl.when(i == 0)
    def _():
        state_s[...] = jnp.zeros_like(state_s)
        hcar_s[...] = jnp.zeros_like(hcar_s)
        p_s[...] = jnp.zeros_like(p_s)

    x = x_ref[...]
    xn = _rms(x, ng_ref[...]).astype(BF16)
    bounds = (0, RWKV_SHIFT_COLS, EVEN_IN_COLS)
    p_parts, full_parts = [], []
    for lo, hi in zip(bounds[:-1], bounds[1:]):
        part = jnp.dot(xn, win_ref[:, lo:hi], preferred_element_type=F32)
        full_parts.append(jnp.concatenate([p_s[:, lo:hi], part], axis=0))
        p_s[:, lo:hi] = part[tm - SUBLANES:tm, :]
        p_parts.append(part)

    def locate(cs):
        idx = 0 if cs.stop <= RWKV_SHIFT_COLS else 1
        return idx, slice(cs.start - bounds[idx], cs.stop - bounds[idx])

    def current(rs, cs):
        idx, loc = locate(cs)
        return p_parts[idx][rs, loc]

    def lagged(rs, cs, lag):
        idx, loc = locate(cs)
        return full_parts[idx][SUBLANES - lag + rs.start:SUBLANES - lag + rs.stop, loc]

    def mixed(rs, cs):
        pa = current(rs, cs)
        return pa + mu_ref[:, cs] * (lagged(rs, cs, 1) - pa)

    blocks = [slice(b * ROW_BLOCK, (b + 1) * ROW_BLOCK) for b in range(tm // ROW_BLOCK)]
    cat = lambda parts: jnp.concatenate(parts, axis=0)
    row = lax.broadcasted_iota(jnp.int32, (tm, tm), 0)
    col = lax.broadcasted_iota(jnp.int32, (tm, tm), 1)
    same = (row // lc) == (col // lc)
    incl = same & (row >= col)
    strict = same & (row > col)
    incl_bf = incl.astype(BF16)
    eye_sbs = (lax.broadcasted_iota(jnp.int32, (lc, tm), 0)
               == lax.broadcasted_iota(jnp.int32, (lc, tm), 1) % lc).astype(F32)

    lane = lax.broadcasted_iota(jnp.int32, (ROW_BLOCK, W_LORA + A_LORA), 1)
    wa_b, gd_b = [], []
    for rs in blocks:
        wa = mixed(rs, slice(3 * rw, 3 * rw + W_LORA + A_LORA))
        wa_b.append(jnp.where(lane < W_LORA, jnp.tanh(wa), wa).astype(BF16))
        gd = mixed(rs, slice(3 * rw + W_LORA + A_LORA, RWKV_SHIFT_COLS))
        gd_b.append(jax.nn.sigmoid(gd).astype(BF16))
    lora = _dot(cat(wa_b), wwa_ref[...])
    g = _dot(cat(gd_b), g2_ref[...])

    def operands(half):
        lo = half * hw
        ch = slice(lo, lo + hw)
        r_b, k_b, v_b = [], [], []
        for rs in blocks:
            r_b.append(mixed(rs, slice(lo, lo + hw)))
            k_b.append(mixed(rs, slice(rw + lo, rw + lo + hw)))
            v_b.append(mixed(rs, slice(2 * rw + lo, 2 * rw + lo + hw)))
            yield
        logw_b, a_b, kk_b, sq_b = [], [], [], []
        for b, rs in enumerate(blocks):
            w_log = -jax.nn.softplus(-(w0_ref[:, ch] + lora[rs, ch])) - 0.5
            logw_b.append(-jnp.exp(w_log))
            a_b.append(jax.nn.sigmoid(a0_ref[:, ch] + lora[rs, rw + lo:rw + lo + hw]))
            kk = k_b[b] * kk_ref[:, ch]
            kk_b.append(kk)
            sq_b.append((kk * kk).astype(BF16))
            yield
        kk_ss = _half_dot(cat(sq_b), hones_ref)
        cum = _dot_split_rhs(incl_bf, cat(logw_b), 2)
        yield
        bon_b, rt_b, zt_b, kt_b, bt_b, kh_b, bh_b, pt_b = [], [], [], [], [], [], [], []
        for b, rs in enumerate(blocks):
            c_end = (rs.start // lc + 1) * lc
            cm = cum[rs]
            tot = cum[c_end - 1:c_end, :]
            a = a_b[b]
            kk = kk_b[b] * jnp.minimum(lax.rsqrt(kk_ss[rs]), 1e12)
            k2 = k_b[b] * (1.0 + (a - 1.0) * ka_ref[:, ch])
            bon_b.append((r_b[b] * k2 * rk_ref[:, ch]).astype(BF16))
            bv = kk * a
            inv_p = jnp.exp(-cm)
            to_end = jnp.exp(tot - cm)
            rt_b.append(r_b[b] * jnp.exp(cm))
            zt_b.append(-kk * jnp.exp(cm - logw_b[b]))
            kt_b.append(k2 * inv_p)
            bt_b.append(bv * inv_p)
            kh_b.append(k2 * to_end)
            bh_b.append(bv * to_end)
            pt_b.append(jnp.broadcast_to(jnp.exp(tot), (ROW_BLOCK, hw)))
            yield
        v = cat(v_b)
        ops = dict(v=v, bonus=_half_dot(cat(bon_b), hones_ref) * v,
                   rt=cat(rt_b), zt=cat(zt_b), kt=cat(kt_b), bt=cat(bt_b))
        yield
        ops.update(kh_t=cat(kh_b).T, bh_t=cat(bh_b).T, pt_t=cat(pt_b).T)
        yield
        return ops

    def side_by_side(m):
        out = m[0:lc]
        for c in range(1, nchunk):
            out = out + m[c * lc:(c + 1) * lc]
        return out

    def block_diag(m):
        return jnp.where(same, jnp.concatenate([m] * nchunk, axis=0), 0.0)

    def wkv(half, ops):
        heads = range(hph)
        hsl = [slice(h * hd, (h + 1) * hd) for h in heads]
        z_h = [ops['zt'][:, s] for s in hsl]
        r_h = [ops['rt'][:, s] for s in hsl]
        v_h = [ops['v'][:, s] for s in hsl]
        kh_t, bh_t, pt_t = ops['kh_t'], ops['bh_t'], ops['pt_t']
        n_zb, a_zk, a_rb, a_rk = [], [], [], []
        for h in heads:
            zr = jnp.concatenate([z_h[h], r_h[h]], axis=0)
            bk = jnp.concatenate([ops['bt'][:, hsl[h]], ops['kt'][:, hsl[h]]], axis=0)
            aa = _dot_nt(zr, bk)
            n_zb.append(jnp.where(strict, aa[0:tm, 0:tm], 0.0))
            a_zk.append(jnp.where(strict, aa[0:tm, tm:2 * tm], 0.0))
            a_rb.append(jnp.where(incl, aa[tm:2 * tm, 0:tm], 0.0))
            a_rk.append(jnp.where(incl, aa[tm:2 * tm, tm:2 * tm], 0.0))
            yield
        xs = [side_by_side(n) for n in n_zb]
        ts = [eye_sbs + x for x in xs]
        for h in heads:
            xs[h] = _dot(xs[h], block_diag(xs[h]))
            yield
        span = 4
        while span < lc:
            for h in heads:
                both = _dot(jnp.concatenate([ts[h], xs[h]], axis=0), block_diag(xs[h]))
                ts[h] = ts[h] + both[0:lc]
                xs[h] = both[lc:2 * lc]
                yield
            span *= 2
        t_inv = [block_diag(ts[h] + _dot(ts[h], block_diag(xs[h]))) for h in heads]
        yield
        wv = [_dot(a_zk[h], v_h[h]) for h in heads]
        yield
        tzu = [_dot(t_inv[h], jnp.concatenate([z_h[h], wv[h]], axis=1)) for h in heads]
        yield
        ar = [_dot(a_rb[h], tzu[h]) for h in heads]
        m_y = [r_h[h] + ar[h][:, 0:hd] for h in heads]
        yield
        y1 = [ar[h][:, hd:2 * hd] + _dot(a_rk[h], v_h[h]) for h in heads]
        yield
        phi, gam = [], []
        for h in heads:
            b_blk = jnp.where(same, jnp.concatenate([bh_t[hsl[h], :]] * nchunk, axis=0), 0.0)
            k_blk = jnp.where(same, jnp.concatenate([kh_t[hsl[h], :]] * nchunk, axis=0), 0.0)
            pg = _dot(b_blk, tzu[h])
            phi.append(pg[:, 0:hd])
            gam.append(pg[:, hd:2 * hd] + _dot(k_blk, v_h[h]))
            yield
        st = [state_s[half * hph + h] for h in heads]
        y_rows = [[] for _ in heads]
        for c in range(nchunk):
            cs = slice(c * lc, (c + 1) * lc)
            for h in heads:
                y_rows[h].append(_dot(m_y[h][cs], st[h]) + y1[h][cs])
                decay = pt_t[hsl[h], c * lc:c * lc + 1]
                st[h] = st[h] * decay + _dot(phi[h][cs], st[h]) + gam[h][cs]
            yield
        for h in heads:
            state_s[half * hph + h] = st[h]
        return jnp.concatenate([jnp.concatenate(rows, axis=0) for rows in y_rows], axis=1)

    def finish(half, y, ops):
        ch = slice(half * hw, (half + 1) * hw)
        mean = _half_dot(y, hones_ref) * (1.0 / hd)
        yc = y - mean
        var = _half_dot(yc * yc, hones_ref) * (1.0 / hd)
        yield
        y = yc * lax.rsqrt(var + GN_EPS) * lnw_ref[:, ch] + lnb_ref[:, ch]
        return (y + ops['bonus']) * g[:, ch]

    def lru():
        c0 = RWKV_SHIFT_COLS
        xc_b = []
        for rs in blocks:
            xc = cb_ref[...]
            for j in range(LRU_CONV):
                xc = xc + cw_ref[j:j + 1, :] * lagged(rs, slice(c0, c0 + LRU_WIDTH), LRU_CONV - 1 - j)
            xc_b.append(xc)
            yield
        xc = cat(xc_b)
        gate_a = _half_dot(xc, gaw_ref) + gab_ref[...]
        gate_x = _half_dot(xc, gxw_ref) + gxb_ref[...]
        yield
        neg_c_softplus = -LRU_C * jax.nn.softplus(-lam_ref[...])
        in_group = lax.broadcasted_iota(jnp.int32, (ROW_BLOCK, LRU_WIDTH), 0) % SUBLANES
        parts = []
        for b, rs in enumerate(blocks):
            log_a = neg_c_softplus * jax.nn.sigmoid(gate_a[rs])
            sa = jnp.exp(log_a)
            su = xc_b[b] * jax.nn.sigmoid(gate_x[rs]) * jnp.sqrt(1.0 - jnp.exp(2.0 * log_a))
            for d in (1, 2, 4):
                keep = in_group >= d
                su, sa = (su + sa * jnp.where(keep, pltpu.roll(su, d, 0), 0.0),
                          sa * jnp.where(keep, pltpu.roll(sa, d, 0), 1.0))
            gate_b = jax.nn.gelu(current(rs, slice(c0 + LRU_WIDTH, c0 + 2 * LRU_WIDTH)))
            parts.append((su, sa, gate_b))
            yield
        h_prev = hcar_s[0:1, :]
        h_groups = []
        for su, sa, gate_b in parts:
            for gidx in range(ROW_BLOCK // SUBLANES):
                gs = slice(gidx * SUBLANES, (gidx + 1) * SUBLANES)
                hg = su[gs] + sa[gs] * h_prev
                h_groups.append(hg * gate_b[gs])
                h_prev = hg[SUBLANES - 1:SUBLANES, :]
            yield
        hcar_s[...] = jnp.broadcast_to(h_prev, hcar_s.shape)
        return jnp.concatenate(h_groups, axis=0)

    ops = _run(operands(0))
    y, y_b = _run_together(wkv(0, ops), lru())
    y_a = _run(finish(0, y, ops))
    y_cat = jnp.concatenate([y_a, y_b], axis=1).astype(BF16)
    o_ref[...] = x + jnp.dot(y_cat, wout_ref[...], preferred_element_type=F32)


def _block_diag(w):
    h, n, _ = w.shape
    eye = jnp.eye(h, dtype=w.dtype)
    return (eye[:, None, :, None] * w[:, :, None, :]).reshape(h * n, h * n)


def _row(v):
    return v.reshape(1, -1).astype(F32)


def _const_spec(shape):
    nd = len(shape)
    return pl.BlockSpec(shape, lambda *_: (0,) * nd, pipeline_mode=pl.Buffered(1))


def _even_layer(x2, bsz, seq, norm_g, w_in, mu, w0, w2, a0, a2, g2, k_k, k_a, r_k, ln_w, ln_b,
                conv_w, conv_b, gate_a_w, gate_a_b, gate_x_w, gate_x_b, lru_lambda, w_out):
    tm = TM_MIX
    nt = seq // tm
    rw = RWKV_WIDTH
    wwa = jnp.zeros((W_LORA + A_LORA, 2 * rw), F32)
    wwa = wwa.at[:W_LORA, :rw].set(w2).at[W_LORA:, rw:].set(a2).astype(BF16)

    def halves(w):
        hh = w.shape[0] // 2
        return jnp.stack([_block_diag(w[:hh]), _block_diag(w[hh:])]).astype(BF16)

    hones = halves(jnp.ones((RWKV_HEADS, RWKV_HEAD, RWKV_HEAD), F32))
    consts = [
        _row(norm_g), w_in.astype(BF16), _row(mu), _row(w0), wwa, _row(a0), g2.astype(BF16),
        _row(k_k), _row(k_a), _row(r_k), _row(ln_w), _row(ln_b), conv_w.astype(F32), _row(conv_b),
        halves(gate_a_w), _row(gate_a_b), halves(gate_x_w),
        _row(gate_x_b), _row(lru_lambda), hones, w_out.astype(BF16),
    ]
    in_specs = [pl.BlockSpec((tm, D_MODEL), lambda b, i: (b * nt + i, 0))]
    in_specs += [_const_spec(c.shape) for c in consts]
    return pl.pallas_call(
        functools.partial(_even_body, tm=tm),
        grid=(bsz, nt),
        in_specs=in_specs,
        out_specs=pl.BlockSpec((tm, D_MODEL), lambda b, i: (b * nt + i, 0)),
        out_shape=jax.ShapeDtypeStruct(x2.shape, F32),
        scratch_shapes=[
            pltpu.VMEM((SUBLANES, EVEN_IN_COLS), F32),
            pltpu.VMEM((RWKV_HEADS, RWKV_HEAD, RWKV_HEAD), F32),
            pltpu.VMEM((SUBLANES, LRU_WIDTH), F32),
        ],
        compiler_params=pltpu.CompilerParams(
            dimension_semantics=("arbitrary", "arbitrary"), vmem_limit_bytes=VMEM_LIMIT),
        name="even_mixer",
    )(x2, *consts)


def _s5_param_body(are_ref, aim_ref, ldt_ref, bre_ref, bim_ref, abre_ref, abim_ref, bbre_ref, bbim_ref):
    lam_re = jnp.minimum(are_ref[...], -1e-4)
    lam_im = aim_ref[...]
    dt = jnp.exp(ldt_ref[...])
    mag = jnp.exp(lam_re * dt)
    ab_re = mag * jnp.cos(lam_im * dt)
    ab_im = mag * jnp.sin(lam_im * dt)
    den = lam_re * lam_re + lam_im * lam_im
    zr = ab_re - 1.0
    q_re = (zr * lam_re + ab_im * lam_im) / den
    q_im = (ab_im * lam_re - zr * lam_im) / den
    abre_ref[...] = ab_re
    abim_ref[...] = ab_im
    for c in range(S5_GROUP):
        bbre_ref[c] = q_re * bre_ref[c] - q_im * bim_ref[c]
        bbim_ref[c] = q_re * bim_ref[c] + q_im * bre_ref[c]


def _s5_params(a_re, a_im, log_dt, b_re, b_im):
    g, n = S5_GROUPS, S5_STATE
    shp = jax.ShapeDtypeStruct
    return pl.pallas_call(
        _s5_param_body,
        out_shape=(shp((g, n), F32), shp((g, n), F32),
                   shp((S5_GROUP, g, n), F32), shp((S5_GROUP, g, n), F32)),
        name="s5_params",
    )(a_re.astype(F32), a_im.astype(F32), log_dt.reshape(g, 1).astype(F32),
      jnp.transpose(b_re, (2, 0, 1)).astype(F32), jnp.transpose(b_im, (2, 0, 1)).astype(F32))


def _odd_body(x_ref, perm_ref, permt_ref, ng_ref, win_ref, bre_ref, bim_ref, cre_ref, cim_ref,
              are_ref, aim_ref, d_ref, wglu_ref, o_ref, sre_s, sim_s, car_s, *, tt):
    i = pl.program_id(1)
    rows = SUBLANES * tt
    nstrip = sre_s.shape[0]
    nblk = bre_ref.shape[0]
    per = nstrip // nblk
    cw = bre_ref.shape[1]

    @pl.when(i == 0)
    def _():
        car_s[...] = jnp.zeros_like(car_s)

    x = x_ref[...].reshape(rows, D_MODEL)
    xn = _rms(x, ng_ref[...]).astype(BF16)
    xn = jnp.dot(perm_ref[...], xn, preferred_element_type=F32).astype(BF16)
    u = jnp.dot(xn, win_ref[...], preferred_element_type=F32)
    ub = u.astype(BF16)
    for j in range(nblk):
        uj = ub[:, j * cw:(j + 1) * cw]
        pr = jnp.dot(uj, bre_ref[j], preferred_element_type=F32)
        pi = jnp.dot(uj, bim_ref[j], preferred_element_type=F32)
        for q in range(per):
            sre_s[j * per + q] = pr[:, q * S5_LANES:(q + 1) * S5_LANES]
            sim_s[j * per + q] = pi[:, q * S5_LANES:(q + 1) * S5_LANES]

    for s in range(nstrip):
        ar = are_ref[s]
        ai = aim_ref[s]
        xr = car_s[0, s]
        xi = car_s[1, s]
        for t in range(tt):
            rs = slice(t * SUBLANES, (t + 1) * SUBLANES)
            xr, xi = (ar * xr - ai * xi + sre_s[s, rs, :], ar * xi + ai * xr + sim_s[s, rs, :])
            sre_s[s, rs, :] = xr
            sim_s[s, rs, :] = xi
        car_s[0, s] = xr
        car_s[1, s] = xi

    ys = []
    for j in range(nblk):
        acc = None
        for q in range(per):
            rsl = slice(q * S5_LANES, (q + 1) * S5_LANES)
            t = (_dot(sre_s[j * per + q], cre_ref[j, rsl, :])
                 - _dot(sim_s[j * per + q], cim_ref[j, rsl, :]))
            acc = t if acc is None else acc + t
        ys.append(acc)
    y = jnp.concatenate(ys, axis=1)
    yy = jax.nn.gelu(y + d_ref[...] * u).astype(BF16)
    yy = jnp.dot(permt_ref[...], yy, preferred_element_type=F32).astype(BF16)
    og = jnp.dot(yy, wglu_ref[...], preferred_element_type=F32)
    out = x + og[:, 0:D_MODEL] * jax.nn.sigmoid(og[:, D_MODEL:2 * D_MODEL])
    o_ref[...] = out.reshape(o_ref.shape)


def _odd_layer(x2, bsz, seq, norm_g, w_in, a_re, a_im, log_dt, b_re, b_im, c_re, c_im, d_skip, w_glu):
    tt = TT_ODD
    assert bsz % SUBLANES == 0 and seq % tt == 0
    nt = seq // tt
    rows = SUBLANES * tt
    g, n, gc = S5_GROUPS, S5_STATE, S5_GROUP
    gpb = 8
    nblk = g // gpb
    nstrip = g * n // S5_LANES
    ab_re, ab_im, bb_re, bb_im = _s5_params(a_re, a_im, log_dt, b_re, b_im)
    eye = jnp.eye(gpb, dtype=F32)

    def b_blocks(bb):
        t = bb.reshape(gc, nblk, gpb, n)
        t = jnp.einsum('cjgn,gh->jgchn', t, eye)
        return t.reshape(nblk, gpb * gc, gpb * n).astype(BF16)

    def c_blocks(cc):
        t = cc.astype(F32).reshape(nblk, gpb, gc, n)
        t = jnp.einsum('jgcn,gh->jgnhc', t, eye)
        return t.reshape(nblk, gpb * n, gpb * gc).astype(BF16)

    def strips(p):
        return jnp.broadcast_to(p.reshape(nstrip, 1, S5_LANES), (nstrip, SUBLANES, S5_LANES))

    dst = jnp.arange(rows)
    src = (dst % SUBLANES) * tt + dst // SUBLANES
    perm = (src[:, None] == jnp.arange(rows)[None, :]).astype(BF16)

    consts = [perm, perm.T, _row(norm_g), w_in.astype(BF16), b_blocks(bb_re), b_blocks(bb_im),
              c_blocks(c_re), c_blocks(c_im), strips(ab_re), strips(ab_im), _row(d_skip),
              w_glu.astype(BF16)]
    x4 = x2.reshape(bsz, nt, tt, D_MODEL)
    xspec = pl.BlockSpec((SUBLANES, 1, tt, D_MODEL), lambda b, i: (b, i, 0, 0))
    out = pl.pallas_call(
        functools.partial(_odd_body, tt=tt),
        grid=(bsz // SUBLANES, nt),
        in_specs=[xspec] + [_const_spec(c.shape) for c in consts],
        out_specs=xspec,
        out_shape=jax.ShapeDtypeStruct(x4.shape, F32),
        scratch_shapes=[
            pltpu.VMEM((nstrip, rows, S5_LANES), F32),
            pltpu.VMEM((nstrip, rows, S5_LANES), F32),
            pltpu.VMEM((2, nstrip, SUBLANES, S5_LANES), F32),
        ],
        compiler_params=pltpu.CompilerParams(
            dimension_semantics=("arbitrary", "arbitrary"), vmem_limit_bytes=VMEM_LIMIT),
        name="odd_mixer",
    )(x4, *consts)
    return out.reshape(x2.shape)


def _ffn_body(x_ref, ng_ref, wup_ref, cw_ref, cb_ref, wd_ref, fg_ref, o_ref,
              h_s, act_s, *, tm, tiles_per_seq, final_norm):
    i = pl.program_id(0)
    tf = TF_FFN
    x = x_ref[...]
    xn = _rms(x, ng_ref[...]).astype(BF16)

    @pl.when(i % tiles_per_seq == 0)
    def _():
        h_s[...] = jnp.zeros_like(h_s)

    def conv(c0):
        h = jnp.dot(xn, wup_ref[:, c0:c0 + tf], preferred_element_type=F32)
        full = jnp.concatenate([h_s[:, c0:c0 + tf], h], axis=0)
        h_s[:, c0:c0 + tf] = h[tm - SUBLANES:tm, :]
        out = cb_ref[:, c0:c0 + tf] + cw_ref[FFN_CONV - 1:FFN_CONV, c0:c0 + tf] * h
        for t in range(FFN_CONV - 1):
            lag = FFN_CONV - 1 - t
            out = out + cw_ref[t:t + 1, c0:c0 + tf] * full[SUBLANES - lag:SUBLANES - lag + tm, :]
        return out

    for c in range(D_FF // tf):
        act = jax.nn.silu(conv(c * tf)) * conv(D_FF + c * tf)
        act_s[:, c * tf:(c + 1) * tf] = act.astype(BF16)
    y = x + jnp.dot(act_s[...], wd_ref[...], preferred_element_type=F32)
    if final_norm:
        y = _rms(y, fg_ref[...])
    o_ref[...] = y


def _layer_spec(stacked, layer):
    nd = stacked.ndim - 1
    return pl.BlockSpec((None,) + stacked.shape[1:], lambda *_: (layer,) + (0,) * nd,
                        pipeline_mode=pl.Buffered(1))


def _ffn_layer(x2, seq, layer, norm_g, w_up_all, conv_w, conv_b, w_down_all, final_g, final_norm):
    tm = TM_FFN
    m = x2.shape[0]
    consts = [_row(norm_g), w_up_all, conv_w.astype(F32), _row(conv_b), w_down_all, _row(final_g)]
    in_specs = [pl.BlockSpec((tm, D_MODEL), lambda i: (i, 0))]
    in_specs += [_layer_spec(c, layer) if c.ndim == 3 else _const_spec(c.shape) for c in consts]
    return pl.pallas_call(
        functools.partial(_ffn_body, tm=tm, tiles_per_seq=seq // tm, final_norm=final_norm),
        grid=(m // tm,),
        in_specs=in_specs,
        out_specs=pl.BlockSpec((tm, D_MODEL), lambda i: (i, 0)),
        out_shape=jax.ShapeDtypeStruct(x2.shape, F32),
        scratch_shapes=[
            pltpu.VMEM((SUBLANES, 2 * D_FF), F32),
            pltpu.VMEM((tm, D_FF), BF16),
        ],
        compiler_params=pltpu.CompilerParams(
            dimension_semantics=("arbitrary",), vmem_limit_bytes=VMEM_LIMIT),
        name="conv_ffn",
    )(x2, *consts)


def kernel(x, e_norm_g, e_w_in, e_mu, e_w0, e_w2, e_a0, e_a2, e_g2, e_k_k, e_k_a, e_r_k,
           e_ln_w, e_ln_b, e_conv_w, e_conv_b, e_gate_a_w, e_gate_a_b, e_gate_x_w, e_gate_x_b,
           e_lru_lambda, e_w_out, o_norm_g, o_w_in, o_A_re, o_A_im, o_log_dt, o_B_re, o_B_im,
           o_C_re, o_C_im, o_D, o_w_glu, f_norm_g, f_w_up, f_conv_w, f_conv_b, f_w_down,
           final_norm_g):
    bsz, seq, dm = x.shape
    assert dm == D_MODEL and seq % TM_FFN == 0 and seq % TM_MIX == 0
    depth = f_norm_g.shape[0]
    h = x.reshape(bsz * seq, dm).astype(F32)
    f_w_up_bf = f_w_up.astype(BF16)
    f_w_down_bf = f_w_down.astype(BF16)
    for i in range(depth):
        j = i // 2
        if i % 2 == 0:
            h = _even_layer(h, bsz, seq, e_norm_g[j], e_w_in[j], e_mu[j], e_w0[j], e_w2[j],
                            e_a0[j], e_a2[j], e_g2[j], e_k_k[j], e_k_a[j], e_r_k[j], e_ln_w[j],
                            e_ln_b[j], e_conv_w[j], e_conv_b[j], e_gate_a_w[j], e_gate_a_b[j],
                            e_gate_x_w[j], e_gate_x_b[j], e_lru_lambda[j], e_w_out[j])
        else:
            h = _odd_layer(h, bsz, seq, o_norm_g[j], o_w_in[j], o_A_re[j], o_A_im[j],
                           o_log_dt[j], o_B_re[j], o_B_im[j], o_C_re[j], o_C_im[j], o_D[j],
                           o_w_glu[j])
        h = _ffn_layer(h, seq, i, f_norm_g[i], f_w_up_bf, f_conv_w[i], f_conv_b[i], f_w_down_bf,
                       final_norm_g, final_norm=(i == depth - 1))
    return h.reshape(bsz, seq, dm).astype(x.dtype)
```

```python
import functools

import jax
import jax.numpy as jnp
from jax import lax
from jax.experimental import pallas as pl
from jax.experimental.pallas import tpu as pltpu

F32 = jnp.float32
BF16 = jnp.bfloat16

D_MODEL = 1024
RWKV_WIDTH = 512
RWKV_HEAD = 64
RWKV_HEADS = 8
W_LORA = 64
A_LORA = 64
G_LORA = 128
LRU_WIDTH = 512
LRU_BLOCKS = 8
LRU_BLOCK = 64
LRU_CONV = 4
LRU_C = 8.0
S5_WIDTH = 1024
S5_GROUP = 16
S5_GROUPS = 64
S5_STATE = 64
D_FF = 2816
FFN_CONV = 3
NORM_EPS = 1e-6
GN_EPS = 64e-5
RWKV_SHIFT_COLS = 3 * RWKV_WIDTH + W_LORA + A_LORA + G_LORA
EVEN_IN_COLS = RWKV_SHIFT_COLS + 2 * LRU_WIDTH

SUBLANES = 8
TM_MIX = 256
WKV_CHUNK = 64
ROW_BLOCK = 64
TT_ODD = 32
S5_LANES = 256
TM_FFN = 1024
TF_FFN = 256
VMEM_LIMIT = 56 * 1024 * 1024


def _dot(a, b):
    return jnp.dot(a.astype(BF16), b.astype(BF16), preferred_element_type=F32)


def _dot_nt(a, b):
    return lax.dot_general(a.astype(BF16), b.astype(BF16), (((1,), (1,)), ((), ())),
                           preferred_element_type=F32)


def _split(x, n):
    parts = []
    for _ in range(n - 1):
        h = x.astype(BF16)
        parts.append(h)
        x = x - h.astype(F32)
    parts.append(x.astype(BF16))
    return parts


def _dot_split_rhs(a_bf16, b, n):
    acc = None
    for piece in _split(b, n):
        t = jnp.dot(a_bf16, piece, preferred_element_type=F32)
        acc = t if acc is None else acc + t
    return acc


def _rms(x, g):
    ms = jnp.mean(x * x, axis=-1, keepdims=True)
    return x * lax.rsqrt(ms + NORM_EPS) * g


def _half_dot(x, w_ref):
    half = x.shape[1] // 2
    return jnp.concatenate([_dot(x[:, 0:half], w_ref[0]), _dot(x[:, half:2 * half], w_ref[1])], axis=1)


def _run(stream):
    while True:
        try:
            next(stream)
        except StopIteration as stop:
            return stop.value


def _run_together(main, side):
    results, done = [None, None], [False, False]
    streams = (main, side)
    while not all(done):
        for idx, stream in enumerate(streams):
            if not done[idx]:
                try:
                    next(stream)
                except StopIteration as stop:
                    results[idx], done[idx] = stop.value, True
    return results


def _chain(*streams):
    out = []
    for stream in streams:
        out.append((yield from stream))
    return out


def _even_tile(t, x_ref, ng_ref, win_ref, mu_ref, w0_ref, wwa_ref, a0_ref, g2_ref,
               kk_ref, ka_ref, rk_ref, lnw_ref, lnb_ref, cw_ref, cb_ref, gaw_ref, gab_ref,
               gxw_ref, gxb_ref, lam_ref, hones_ref, wout_ref, p_s, state_s, hcar_s, *, tm):
    rw = RWKV_WIDTH
    hw = rw
    hd = RWKV_HEAD
    hph = RWKV_HEADS
    lc = WKV_CHUNK
    nchunk = tm // lc

    x = x_ref[t]
    xn = _rms(x, ng_ref[...]).astype(BF16)
    bounds = (0, rw, 2 * rw, 3 * rw, RWKV_SHIFT_COLS, RWKV_SHIFT_COLS + LRU_WIDTH, EVEN_IN_COLS)
    p_parts, full_parts = [], []

    def inproj():
        for lo, hi in zip(bounds[:-1], bounds[1:]):
            part = jnp.dot(xn, win_ref[:, lo:hi], preferred_element_type=F32)
            full_parts.append(jnp.concatenate([p_s[t, :, lo:hi], part], axis=0))
            p_s[t, :, lo:hi] = part[tm - SUBLANES:tm, :]
            p_parts.append(part)
            yield

    def locate(cs):
        idx = max(n for n in range(len(bounds) - 1) if bounds[n] <= cs.start)
        assert cs.stop <= bounds[idx + 1]
        return idx, slice(cs.start - bounds[idx], cs.stop - bounds[idx])

    def current(rs, cs):
        idx, loc = locate(cs)
        return p_parts[idx][rs, loc]

    def lagged(rs, cs, lag):
        idx, loc = locate(cs)
        return full_parts[idx][SUBLANES - lag + rs.start:SUBLANES - lag + rs.stop, loc]

    def mixed(rs, cs):
        pa = current(rs, cs)
        return pa + mu_ref[:, cs] * (lagged(rs, cs, 1) - pa)

    blocks = [slice(b * ROW_BLOCK, (b + 1) * ROW_BLOCK) for b in range(tm // ROW_BLOCK)]
    cat = lambda parts: jnp.concatenate(parts, axis=0)
    row = lax.broadcasted_iota(jnp.int32, (tm, tm), 0)
    col = lax.broadcasted_iota(jnp.int32, (tm, tm), 1)
    same = (row // lc) == (col // lc)
    incl = same & (row >= col)
    strict = same & (row > col)
    incl_bf = incl.astype(BF16)
    eye_sbs = (lax.broadcasted_iota(jnp.int32, (lc, tm), 0)
               == lax.broadcasted_iota(jnp.int32, (lc, tm), 1) % lc).astype(F32)

    def operands(half):
        lo = half * hw
        ch = slice(lo, lo + hw)
        lane = lax.broadcasted_iota(jnp.int32, (ROW_BLOCK, W_LORA + A_LORA), 1)
        wa_b, gd_b = [], []
        for rs in blocks:
            wa = mixed(rs, slice(3 * rw, 3 * rw + W_LORA + A_LORA))
            wa_b.append(jnp.where(lane < W_LORA, jnp.tanh(wa), wa).astype(BF16))
            gd = mixed(rs, slice(3 * rw + W_LORA + A_LORA, RWKV_SHIFT_COLS))
            gd_b.append(jax.nn.sigmoid(gd).astype(BF16))
            yield
        lora = _dot(cat(wa_b), wwa_ref[...])
        g = _dot(cat(gd_b), g2_ref[...])
        yield
        r_b, k_b, v_b = [], [], []
        for rs in blocks:
            r_b.append(mixed(rs, slice(lo, lo + hw)))
            k_b.append(mixed(rs, slice(rw + lo, rw + lo + hw)))
            v_b.append(mixed(rs, slice(2 * rw + lo, 2 * rw + lo + hw)))
            yield
        logw_b, a_b, kk_b, sq_b = [], [], [], []
        for b, rs in enumerate(blocks):
            w_log = -jax.nn.softplus(-(w0_ref[:, ch] + lora[rs, ch])) - 0.5
            logw_b.append(-jnp.exp(w_log))
            a_b.append(jax.nn.sigmoid(a0_ref[:, ch] + lora[rs, rw + lo:rw + lo + hw]))
            kk = k_b[b] * kk_ref[:, ch]
            kk_b.append(kk)
            sq_b.append((kk * kk).astype(BF16))
            yield
        kk_ss = _half_dot(cat(sq_b), hones_ref)
        cum = _dot_split_rhs(incl_bf, cat(logw_b), 2)
        yield
        bon_b, rt_b, zt_b, kt_b, bt_b, kh_b, bh_b, pt_b = [], [], [], [], [], [], [], []
        for b, rs in enumerate(blocks):
            c_end = (rs.start // lc + 1) * lc
            cm = cum[rs]
            tot = cum[c_end - 1:c_end, :]
            a = a_b[b]
            kk = kk_b[b] * jnp.minimum(lax.rsqrt(kk_ss[rs]), 1e12)
            k2 = k_b[b] * (1.0 + (a - 1.0) * ka_ref[:, ch])
            bon_b.append((r_b[b] * k2 * rk_ref[:, ch]).astype(BF16))
            bv = kk * a
            inv_p = jnp.exp(-cm)
            to_end = jnp.exp(tot - cm)
            rt_b.append(r_b[b] * jnp.exp(cm))
            zt_b.append(-kk * jnp.exp(cm - logw_b[b]))
            kt_b.append(k2 * inv_p)
            bt_b.append(bv * inv_p)
            kh_b.append(k2 * to_end)
            bh_b.append(bv * to_end)
            pt_b.append(jnp.broadcast_to(jnp.exp(tot), (ROW_BLOCK, hw)))
            yield
        v = cat(v_b)
        ops = dict(v=v, g=g, bonus=_half_dot(cat(bon_b), hones_ref) * v,
                   rt=cat(rt_b), zt=cat(zt_b), kt=cat(kt_b), bt=cat(bt_b))
        yield
        ops.update(kh_t=cat(kh_b).T, bh_t=cat(bh_b).T, pt_t=cat(pt_b).T)
        yield
        return ops

    def side_by_side(m):
        out = m[0:lc]
        for c in range(1, nchunk):
            out = out + m[c * lc:(c + 1) * lc]
        return out

    def block_diag(m):
        return jnp.where(same, jnp.concatenate([m] * nchunk, axis=0), 0.0)

    def wkv(half, ops):
        heads = range(hph)
        hsl = [slice(h * hd, (h + 1) * hd) for h in heads]
        z_h = [ops['zt'][:, s] for s in hsl]
        r_h = [ops['rt'][:, s] for s in hsl]
        v_h = [ops['v'][:, s] for s in hsl]
        kh_t, bh_t, pt_t = ops['kh_t'], ops['bh_t'], ops['pt_t']
        n_zb, a_zk, a_rb, a_rk = [], [], [], []
        for h in heads:
            zr = jnp.concatenate([z_h[h], r_h[h]], axis=0)
            bk = jnp.concatenate([ops['bt'][:, hsl[h]], ops['kt'][:, hsl[h]]], axis=0)
            aa = _dot_nt(zr, bk)
            n_zb.append(jnp.where(strict, aa[0:tm, 0:tm], 0.0))
            a_zk.append(jnp.where(strict, aa[0:tm, tm:2 * tm], 0.0))
            a_rb.append(jnp.where(incl, aa[tm:2 * tm, 0:tm], 0.0))
            a_rk.append(jnp.where(incl, aa[tm:2 * tm, tm:2 * tm], 0.0))
            yield
        xs = [side_by_side(n) for n in n_zb]
        ts = [eye_sbs + x for x in xs]
        for h in heads:
            xs[h] = _dot(xs[h], block_diag(xs[h]))
            yield
        span = 4
        while span < lc:
            for h in heads:
                both = _dot(jnp.concatenate([ts[h], xs[h]], axis=0), block_diag(xs[h]))
                ts[h] = ts[h] + both[0:lc]
                xs[h] = both[lc:2 * lc]
                yield
            span *= 2
        t_inv = [block_diag(ts[h] + _dot(ts[h], block_diag(xs[h]))) for h in heads]
        yield
        wv = [_dot(a_zk[h], v_h[h]) for h in heads]
        yield
        tzu = [_dot(t_inv[h], jnp.concatenate([z_h[h], wv[h]], axis=1)) for h in heads]
        yield
        ar = [_dot(a_rb[h], tzu[h]) for h in heads]
        m_y = [r_h[h] + ar[h][:, 0:hd] for h in heads]
        yield
        y1 = [ar[h][:, hd:2 * hd] + _dot(a_rk[h], v_h[h]) for h in heads]
        yield
        phi, gam = [], []
        for h in heads:
            b_blk = jnp.where(same, jnp.concatenate([bh_t[hsl[h], :]] * nchunk, axis=0), 0.0)
            k_blk = jnp.where(same, jnp.concatenate([kh_t[hsl[h], :]] * nchunk, axis=0), 0.0)
            pg = _dot(b_blk, tzu[h])
            phi.append(pg[:, 0:hd])
            gam.append(pg[:, hd:2 * hd] + _dot(k_blk, v_h[h]))
            yield
        st = [state_s[t, half * hph + h] for h in heads]
        y_rows = [[] for _ in heads]
        for c in range(nchunk):
            cs = slice(c * lc, (c + 1) * lc)
            for h in heads:
                y_rows[h].append(_dot(m_y[h][cs], st[h]) + y1[h][cs])
                decay = pt_t[hsl[h], c * lc:c * lc + 1]
                st[h] = st[h] * decay + _dot(phi[h][cs], st[h]) + gam[h][cs]
            yield
        for h in heads:
            state_s[t, half * hph + h] = st[h]
        return jnp.concatenate([jnp.concatenate(rows, axis=0) for rows in y_rows], axis=1)

    def finish(half, y, ops):
        ch = slice(half * hw, (half + 1) * hw)
        mean = _half_dot(y, hones_ref) * (1.0 / hd)
        yc = y - mean
        var = _half_dot(yc * yc, hones_ref) * (1.0 / hd)
        yield
        y = yc * lax.rsqrt(var + GN_EPS) * lnw_ref[:, ch] + lnb_ref[:, ch]
        return (y + ops['bonus']) * ops['g'][:, ch]

    def lru():
        c0 = RWKV_SHIFT_COLS
        xc_b = []
        for rs in blocks:
            xc = cb_ref[...]
            for j in range(LRU_CONV):
                xc = xc + cw_ref[j:j + 1, :] * lagged(rs, slice(c0, c0 + LRU_WIDTH), LRU_CONV - 1 - j)
            xc_b.append(xc)
            yield
        xc = cat(xc_b)
        gate_a = _half_dot(xc, gaw_ref) + gab_ref[...]
        gate_x = _half_dot(xc, gxw_ref) + gxb_ref[...]
        yield
        neg_c_softplus = -LRU_C * jax.nn.softplus(-lam_ref[...])
        in_group = lax.broadcasted_iota(jnp.int32, (ROW_BLOCK, LRU_WIDTH), 0) % SUBLANES
        parts = []
        for b, rs in enumerate(blocks):
            log_a = neg_c_softplus * jax.nn.sigmoid(gate_a[rs])
            sa = jnp.exp(log_a)
            su = xc_b[b] * jax.nn.sigmoid(gate_x[rs]) * jnp.sqrt(1.0 - jnp.exp(2.0 * log_a))
            for d in (1, 2, 4):
                keep = in_group >= d
                su, sa = (su + sa * jnp.where(keep, pltpu.roll(su, d, 0), 0.0),
                          sa * jnp.where(keep, pltpu.roll(sa, d, 0), 1.0))
            gate_b = jax.nn.gelu(current(rs, slice(c0 + LRU_WIDTH, c0 + 2 * LRU_WIDTH)))
            parts.append((su, sa, gate_b))
            yield
        h_prev = hcar_s[t, 0:1, :]
        h_groups = []
        for su, sa, gate_b in parts:
            for gidx in range(ROW_BLOCK // SUBLANES):
                gs = slice(gidx * SUBLANES, (gidx + 1) * SUBLANES)
                hg = su[gs] + sa[gs] * h_prev
                h_groups.append(hg * gate_b[gs])
                h_prev = hg[SUBLANES - 1:SUBLANES, :]
            yield
        hcar_s[t] = jnp.broadcast_to(h_prev, hcar_s.shape[1:])
        return jnp.concatenate(h_groups, axis=0)

    def output(y_a, y_b):
        y_cat = jnp.concatenate([y_a, y_b], axis=1).astype(BF16)
        return x + jnp.dot(y_cat, wout_ref[...], preferred_element_type=F32)

    return dict(inproj=inproj, operands=lambda: operands(0), wkv=lambda ops: wkv(0, ops), lru=lru,
                finish=lambda y, ops: finish(0, y, ops), output=output)


def _even_body(x_ref, *refs, tm):
    o_ref, p_s, state_s, hcar_s = refs[-4:]
    i = pl.program_id(1)

    @pl.when(i == 0)
    def _():
        state_s[...] = jnp.zeros_like(state_s)
        hcar_s[...] = jnp.zeros_like(hcar_s)
        p_s[...] = jnp.zeros_like(p_s)

    t0, t1 = [_even_tile(t, x_ref, *refs[:-4], p_s, state_s, hcar_s, tm=tm) for t in range(2)]
    _run(t0['inproj']())
    ops0, _ = _run_together(t0['operands'](), t1['inproj']())
    y0, ops1 = _run_together(t0['wkv'](ops0), t1['operands']())
    y1, (yb0, ya0, yb1) = _run_together(
        t1['wkv'](ops1), _chain(t0['lru'](), t0['finish'](y0, ops0), t1['lru']()))
    o_ref[0] = t0['output'](ya0, yb0)
    ya1 = _run(t1['finish'](y1, ops1))
    o_ref[1] = t1['output'](ya1, yb1)


def _block_diag(w):
    h, n, _ = w.shape
    eye = jnp.eye(h, dtype=w.dtype)
    return (eye[:, None, :, None] * w[:, :, None, :]).reshape(h * n, h * n)


def _row(v):
    return v.reshape(1, -1).astype(F32)


def _const_spec(shape):
    nd = len(shape)
    return pl.BlockSpec(shape, lambda *_: (0,) * nd, pipeline_mode=pl.Buffered(1))


def _even_layer(x2, bsz, seq, norm_g, w_in, mu, w0, w2, a0, a2, g2, k_k, k_a, r_k, ln_w, ln_b,
                conv_w, conv_b, gate_a_w, gate_a_b, gate_x_w, gate_x_b, lru_lambda, w_out):
    tm = TM_MIX
    nt = seq // tm
    rw = RWKV_WIDTH
    wwa = jnp.zeros((W_LORA + A_LORA, 2 * rw), F32)
    wwa = wwa.at[:W_LORA, :rw].set(w2).at[W_LORA:, rw:].set(a2).astype(BF16)

    def halves(w):
        hh = w.shape[0] // 2
        return jnp.stack([_block_diag(w[:hh]), _block_diag(w[hh:])]).astype(BF16)

    hones = halves(jnp.ones((RWKV_HEADS, RWKV_HEAD, RWKV_HEAD), F32))
    consts = [
        _row(norm_g), w_in.astype(BF16), _row(mu), _row(w0), wwa, _row(a0), g2.astype(BF16),
        _row(k_k), _row(k_a), _row(r_k), _row(ln_w), _row(ln_b), conv_w.astype(F32), _row(conv_b),
        halves(gate_a_w), _row(gate_a_b), halves(gate_x_w),
        _row(gate_x_b), _row(lru_lambda), hones, w_out.astype(BF16),
    ]
    assert bsz % 2 == 0
    x3 = x2.reshape(2, (bsz // 2) * seq, D_MODEL)
    xspec = pl.BlockSpec((2, tm, D_MODEL), lambda b, i: (0, b * nt + i, 0))
    out = pl.pallas_call(
        functools.partial(_even_body, tm=tm),
        grid=(bsz // 2, nt),
        in_specs=[xspec] + [_const_spec(c.shape) for c in consts],
        out_specs=xspec,
        out_shape=jax.ShapeDtypeStruct(x3.shape, F32),
        scratch_shapes=[
            pltpu.VMEM((2, SUBLANES, EVEN_IN_COLS), F32),
            pltpu.VMEM((2, RWKV_HEADS, RWKV_HEAD, RWKV_HEAD), F32),
            pltpu.VMEM((2, SUBLANES, LRU_WIDTH), F32),
        ],
        compiler_params=pltpu.CompilerParams(
            dimension_semantics=("arbitrary", "arbitrary"), vmem_limit_bytes=VMEM_LIMIT),
        name="even_mixer",
    )(x3, *consts)
    return out.reshape(x2.shape)


def _s5_param_body(are_ref, aim_ref, ldt_ref, bre_ref, bim_ref, abre_ref, abim_ref, bbre_ref, bbim_ref):
    lam_re = jnp.minimum(are_ref[...], -1e-4)
    lam_im = aim_ref[...]
    dt = jnp.exp(ldt_ref[...])
    mag = jnp.exp(lam_re * dt)
    ab_re = mag * jnp.cos(lam_im * dt)
    ab_im = mag * jnp.sin(lam_im * dt)
    den = lam_re * lam_re + lam_im * lam_im
    zr = ab_re - 1.0
    q_re = (zr * lam_re + ab_im * lam_im) / den
    q_im = (ab_im * lam_re - zr * lam_im) / den
    abre_ref[...] = ab_re
    abim_ref[...] = ab_im
    for c in range(S5_GROUP):
        bbre_ref[c] = q_re * bre_ref[c] - q_im * bim_ref[c]
        bbim_ref[c] = q_re * bim_ref[c] + q_im * bre_ref[c]


def _s5_params(a_re, a_im, log_dt, b_re, b_im):
    g, n = S5_GROUPS, S5_STATE
    shp = jax.ShapeDtypeStruct
    return pl.pallas_call(
        _s5_param_body,
        out_shape=(shp((g, n), F32), shp((g, n), F32),
                   shp((S5_GROUP, g, n), F32), shp((S5_GROUP, g, n), F32)),
        name="s5_params",
    )(a_re.astype(F32), a_im.astype(F32), log_dt.reshape(g, 1).astype(F32),
      jnp.transpose(b_re, (2, 0, 1)).astype(F32), jnp.transpose(b_im, (2, 0, 1)).astype(F32))


def _odd_body(x_ref, perm_ref, permt_ref, ng_ref, win_ref, bre_ref, bim_ref, cre_ref, cim_ref,
              are_ref, aim_ref, d_ref, wglu_ref, o_ref, sre_s, sim_s, car_s, *, tt):
    i = pl.program_id(1)
    rows = SUBLANES * tt
    nstrip = sre_s.shape[0]
    nblk = bre_ref.shape[0]
    per = nstrip // nblk
    cw = bre_ref.shape[1]

    @pl.when(i == 0)
    def _():
        car_s[...] = jnp.zeros_like(car_s)

    x = x_ref[...].reshape(rows, D_MODEL)
    xn = _rms(x, ng_ref[...]).astype(BF16)
    xn = jnp.dot(perm_ref[...], xn, preferred_element_type=F32).astype(BF16)
    u = jnp.dot(xn, win_ref[...], preferred_element_type=F32)
    ub = u.astype(BF16)
    for j in range(nblk):
        uj = ub[:, j * cw:(j + 1) * cw]
        pr = jnp.dot(uj, bre_ref[j], preferred_element_type=F32)
        pi = jnp.dot(uj, bim_ref[j], preferred_element_type=F32)
        for q in range(per):
            sre_s[j * per + q] = pr[:, q * S5_LANES:(q + 1) * S5_LANES]
            sim_s[j * per + q] = pi[:, q * S5_LANES:(q + 1) * S5_LANES]

    for s in range(nstrip):
        ar = are_ref[s]
        ai = aim_ref[s]
        xr = car_s[0, s]
        xi = car_s[1, s]
        for t in range(tt):
            rs = slice(t * SUBLANES, (t + 1) * SUBLANES)
            xr, xi = (ar * xr - ai * xi + sre_s[s, rs, :], ar * xi + ai * xr + sim_s[s, rs, :])
            sre_s[s, rs, :] = xr
            sim_s[s, rs, :] = xi
        car_s[0, s] = xr
        car_s[1, s] = xi

    ys = []
    for j in range(nblk):
        acc = None
        for q in range(per):
            rsl = slice(q * S5_LANES, (q + 1) * S5_LANES)
            t = (_dot(sre_s[j * per + q], cre_ref[j, rsl, :])
                 - _dot(sim_s[j * per + q], cim_ref[j, rsl, :]))
            acc = t if acc is None else acc + t
        ys.append(acc)
    y = jnp.concatenate(ys, axis=1)
    yy = jax.nn.gelu(y + d_ref[...] * u).astype(BF16)
    yy = jnp.dot(permt_ref[...], yy, preferred_element_type=F32).astype(BF16)
    og = jnp.dot(yy, wglu_ref[...], preferred_element_type=F32)
    out = x + og[:, 0:D_MODEL] * jax.nn.sigmoid(og[:, D_MODEL:2 * D_MODEL])
    o_ref[...] = out.reshape(o_ref.shape)


def _odd_layer(x2, bsz, seq, norm_g, w_in, a_re, a_im, log_dt, b_re, b_im, c_re, c_im, d_skip, w_glu):
    tt = TT_ODD
    assert bsz % SUBLANES == 0 and seq % tt == 0
    nt = seq // tt
    rows = SUBLANES * tt
    g, n, gc = S5_GROUPS, S5_STATE, S5_GROUP
    gpb = 8
    nblk = g // gpb
    nstrip = g * n // S5_LANES
    ab_re, ab_im, bb_re, bb_im = _s5_params(a_re, a_im, log_dt, b_re, b_im)
    eye = jnp.eye(gpb, dtype=F32)

    def b_blocks(bb):
        t = bb.reshape(gc, nblk, gpb, n)
        t = jnp.einsum('cjgn,gh->jgchn', t, eye)
        return t.reshape(nblk, gpb * gc, gpb * n).astype(BF16)

    def c_blocks(cc):
        t = cc.astype(F32).reshape(nblk, gpb, gc, n)
        t = jnp.einsum('jgcn,gh->jgnhc', t, eye)
        return t.reshape(nblk, gpb * n, gpb * gc).astype(BF16)

    def strips(p):
        return jnp.broadcast_to(p.reshape(nstrip, 1, S5_LANES), (nstrip, SUBLANES, S5_LANES))

    dst = jnp.arange(rows)
    src = (dst % SUBLANES) * tt + dst // SUBLANES
    perm = (src[:, None] == jnp.arange(rows)[None, :]).astype(BF16)

    consts = [perm, perm.T, _row(norm_g), w_in.astype(BF16), b_blocks(bb_re), b_blocks(bb_im),
              c_blocks(c_re), c_blocks(c_im), strips(ab_re), strips(ab_im), _row(d_skip),
              w_glu.astype(BF16)]
    x4 = x2.reshape(bsz, nt, tt, D_MODEL)
    xspec = pl.BlockSpec((SUBLANES, 1, tt, D_MODEL), lambda b, i: (b, i, 0, 0))
    out = pl.pallas_call(
        functools.partial(_odd_body, tt=tt),
        grid=(bsz // SUBLANES, nt),
        in_specs=[xspec] + [_const_spec(c.shape) for c in consts],
        out_specs=xspec,
        out_shape=jax.ShapeDtypeStruct(x4.shape, F32),
        scratch_shapes=[
            pltpu.VMEM((nstrip, rows, S5_LANES), F32),
            pltpu.VMEM((nstrip, rows, S5_LANES), F32),
            pltpu.VMEM((2, nstrip, SUBLANES, S5_LANES), F32),
        ],
        compiler_params=pltpu.CompilerParams(
            dimension_semantics=("arbitrary", "arbitrary"), vmem_limit_bytes=VMEM_LIMIT),
        name="odd_mixer",
    )(x4, *consts)
    return out.reshape(x2.shape)


def _ffn_body(x_ref, ng_ref, wup_ref, cw_ref, cb_ref, wd_ref, fg_ref, o_ref,
              h_s, act_s, *, tm, tiles_per_seq, final_norm):
    i = pl.program_id(0)
    tf = TF_FFN
    x = x_ref[...]
    xn = _rms(x, ng_ref[...]).astype(BF16)

    @pl.when(i % tiles_per_seq == 0)
    def _():
        h_s[...] = jnp.zeros_like(h_s)

    def conv(c0):
        h = jnp.dot(xn, wup_ref[:, c0:c0 + tf], preferred_element_type=F32)
        full = jnp.concatenate([h_s[:, c0:c0 + tf], h], axis=0)
        h_s[:, c0:c0 + tf] = h[tm - SUBLANES:tm, :]
        out = cb_ref[:, c0:c0 + tf] + cw_ref[FFN_CONV - 1:FFN_CONV, c0:c0 + tf] * h
        for t in range(FFN_CONV - 1):
            lag = FFN_CONV - 1 - t
            out = out + cw_ref[t:t + 1, c0:c0 + tf] * full[SUBLANES - lag:SUBLANES - lag + tm, :]
        return out

    for c in range(D_FF // tf):
        act = jax.nn.silu(conv(c * tf)) * conv(D_FF + c * tf)
        act_s[:, c * tf:(c + 1) * tf] = act.astype(BF16)
    y = x + jnp.dot(act_s[...], wd_ref[...], preferred_element_type=F32)
    if final_norm:
        y = _rms(y, fg_ref[...])
    o_ref[...] = y


def _layer_spec(stacked, layer):
    nd = stacked.ndim - 1
    return pl.BlockSpec((None,) + stacked.shape[1:], lambda *_: (layer,) + (0,) * nd,
                        pipeline_mode=pl.Buffered(1))


def _ffn_layer(x2, seq, layer, norm_g, w_up_all, conv_w, conv_b, w_down_all, final_g, final_norm):
    tm = TM_FFN
    m = x2.shape[0]
    consts = [_row(norm_g), w_up_all, conv_w.astype(F32), _row(conv_b), w_down_all, _row(final_g)]
    in_specs = [pl.BlockSpec((tm, D_MODEL), lambda i: (i, 0))]
    in_specs += [_layer_spec(c, layer) if c.ndim == 3 else _const_spec(c.shape) for c in consts]
    return pl.pallas_call(
        functools.partial(_ffn_body, tm=tm, tiles_per_seq=seq // tm, final_norm=final_norm),
        grid=(m // tm,),
        in_specs=in_specs,
        out_specs=pl.BlockSpec((tm, D_MODEL), lambda i: (i, 0)),
        out_shape=jax.ShapeDtypeStruct(x2.shape, F32),
        scratch_shapes=[
            pltpu.VMEM((SUBLANES, 2 * D_FF), F32),
            pltpu.VMEM((tm, D_FF), BF16),
        ],
        compiler_params=pltpu.CompilerParams(
            dimension_semantics=("arbitrary",), vmem_limit_bytes=VMEM_LIMIT),
        name="conv_ffn",
    )(x2, *consts)


def kernel(x, e_norm_g, e_w_in, e_mu, e_w0, e_w2, e_a0, e_a2, e_g2, e_k_k, e_k_a, e_r_k,
           e_ln_w, e_ln_b, e_conv_w, e_conv_b, e_gate_a_w, e_gate_a_b, e_gate_x_w, e_gate_x_b,
           e_lru_lambda, e_w_out, o_norm_g, o_w_in, o_A_re, o_A_im, o_log_dt, o_B_re, o_B_im,
           o_C_re, o_C_im, o_D, o_w_glu, f_norm_g, f_w_up, f_conv_w, f_conv_b, f_w_down,
           final_norm_g):
    bsz, seq, dm = x.shape
    assert dm == D_MODEL and seq % TM_FFN == 0 and seq % TM_MIX == 0
    depth = f_norm_g.shape[0]
    h = x.reshape(bsz * seq, dm).astype(F32)
    f_w_up_bf = f_w_up.astype(BF16)
    f_w_down_bf = f_w_down.astype(BF16)
    for i in range(depth):
        j = i // 2
        if i % 2 == 0:
            h = _even_layer(h, bsz, seq, e_norm_g[j], e_w_in[j], e_mu[j], e_w0[j], e_w2[j],
                            e_a0[j], e_a2[j], e_g2[j], e_k_k[j], e_k_a[j], e_r_k[j], e_ln_w[j],
                            e_ln_b[j], e_conv_w[j], e_conv_b[j], e_gate_a_w[j], e_gate_a_b[j],
                            e_gate_x_w[j], e_gate_x_b[j], e_lru_lambda[j], e_w_out[j])
        else:
            h = _odd_layer(h, bsz, seq, o_norm_g[j], o_w_in[j], o_A_re[j], o_A_im[j],
                           o_log_dt[j], o_B_re[j], o_B_im[j], o_C_re[j], o_C_im[j], o_D[j],
                           o_w_glu[j])
        h = _ffn_layer(h, seq, i, f_norm_g[i], f_w_up_bf, f_conv_w[i], f_conv_b[i], f_w_down_bf,
                       final_norm_g, final_norm=(i == depth - 1))
    return h.reshape(bsz, seq, dm).astype(x.dtype)
```

```python
import functools

import jax
import jax.numpy as jnp
from jax import lax
from jax.experimental import pallas as pl
from jax.experimental.pallas import tpu as pltpu

F32 = jnp.float32
BF16 = jnp.bfloat16

D_MODEL = 1024
RWKV_WIDTH = 512
RWKV_HEAD = 64
RWKV_HEADS = 8
W_LORA = 64
A_LORA = 64
G_LORA = 128
LRU_WIDTH = 512
LRU_BLOCKS = 8
LRU_BLOCK = 64
LRU_CONV = 4
LRU_C = 8.0
S5_WIDTH = 1024
S5_GROUP = 16
S5_GROUPS = 64
S5_STATE = 64
D_FF = 2816
FFN_CONV = 3
NORM_EPS = 1e-6
GN_EPS = 64e-5
RWKV_SHIFT_COLS = 3 * RWKV_WIDTH + W_LORA + A_LORA + G_LORA
EVEN_IN_COLS = RWKV_SHIFT_COLS + 2 * LRU_WIDTH

SUBLANES = 8
TM_MIX = 256
WKV_CHUNK = 64
ROW_BLOCK = 64
TT_ODD = 32
S5_LANES = 256
TM_FFN = 1024
TF_FFN = 256
VMEM_LIMIT = 56 * 1024 * 1024


def _dot(a, b):
    return jnp.dot(a.astype(BF16), b.astype(BF16), preferred_element_type=F32)


def _dot_nt(a, b):
    return lax.dot_general(a.astype(BF16), b.astype(BF16), (((1,), (1,)), ((), ())),
                           preferred_element_type=F32)


def _split(x, n):
    parts = []
    for _ in range(n - 1):
        h = x.astype(BF16)
        parts.append(h)
        x = x - h.astype(F32)
    parts.append(x.astype(BF16))
    return parts


def _dot_split_rhs(a_bf16, b, n):
    acc = None
    for piece in _split(b, n):
        t = jnp.dot(a_bf16, piece, preferred_element_type=F32)
        acc = t if acc is None else acc + t
    return acc


def _rms(x, g):
    ms = jnp.mean(x * x, axis=-1, keepdims=True)
    return x * lax.rsqrt(ms + NORM_EPS) * g


def _half_dot(x, w_ref):
    half = x.shape[1] // 2
    return jnp.concatenate([_dot(x[:, 0:half], w_ref[0]), _dot(x[:, half:2 * half], w_ref[1])], axis=1)


def _run(stream):
    while True:
        try:
            next(stream)
        except StopIteration as stop:
            return stop.value


def _run_together(main, side):
    results, done = [None, None], [False, False]
    streams = (main, side)
    while not all(done):
        for idx, stream in enumerate(streams):
            if not done[idx]:
                try:
                    next(stream)
                except StopIteration as stop:
                    results[idx], done[idx] = stop.value, True
    return results


def _chain(*streams):
    out = []
    for stream in streams:
        out.append((yield from stream))
    return out


def _even_tile(t, x_ref, ng_ref, win_ref, mu_ref, w0_ref, wwa_ref, a0_ref, g2_ref,
               kk_ref, ka_ref, rk_ref, lnw_ref, lnb_ref, cw_ref, cb_ref, gaw_ref, gab_ref,
               gxw_ref, gxb_ref, lam_ref, hones_ref, wout_ref, p_s, state_s, hcar_s, *, tm):
    rw = RWKV_WIDTH
    hw = rw
    hd = RWKV_HEAD
    hph = RWKV_HEADS
    lc = WKV_CHUNK
    nchunk = tm // lc

    x = x_ref[t]
    xn = _rms(x, ng_ref[...]).astype(BF16)
    bounds = (0, rw, 2 * rw, 3 * rw, RWKV_SHIFT_COLS, RWKV_SHIFT_COLS + LRU_WIDTH, EVEN_IN_COLS)
    p_parts, full_parts = [], []

    def inproj():
        for lo, hi in zip(bounds[:-1], bounds[1:]):
            part = jnp.dot(xn, win_ref[:, lo:hi], preferred_element_type=F32)
            full_parts.append(jnp.concatenate([p_s[t, :, lo:hi], part], axis=0))
            p_s[t, :, lo:hi] = part[tm - SUBLANES:tm, :]
            p_parts.append(part)
            yield

    def locate(cs):
        idx = max(n for n in range(len(bounds) - 1) if bounds[n] <= cs.start)
        assert cs.stop <= bounds[idx + 1]
        return idx, slice(cs.start - bounds[idx], cs.stop - bounds[idx])

    def current(rs, cs):
        idx, loc = locate(cs)
        return p_parts[idx][rs, loc]

    def lagged(rs, cs, lag):
        idx, loc = locate(cs)
        return full_parts[idx][SUBLANES - lag + rs.start:SUBLANES - lag + rs.stop, loc]

    def mixed(rs, cs):
        pa = current(rs, cs)
        return pa + mu_ref[:, cs] * (lagged(rs, cs, 1) - pa)

    blocks = [slice(b * ROW_BLOCK, (b + 1) * ROW_BLOCK) for b in range(tm // ROW_BLOCK)]
    cat = lambda parts: jnp.concatenate(parts, axis=0)
    row = lax.broadcasted_iota(jnp.int32, (tm, tm), 0)
    col = lax.broadcasted_iota(jnp.int32, (tm, tm), 1)
    same = (row // lc) == (col // lc)
    incl = same & (row >= col)
    strict = same & (row > col)
    incl_bf = incl.astype(BF16)
    eye_sbs = (lax.broadcasted_iota(jnp.int32, (lc, tm), 0)
               == lax.broadcasted_iota(jnp.int32, (lc, tm), 1) % lc).astype(F32)

    def operands(half):
        lo = half * hw
        ch = slice(lo, lo + hw)
        lane = lax.broadcasted_iota(jnp.int32, (ROW_BLOCK, W_LORA + A_LORA), 1)
        wa_b, gd_b = [], []
        for rs in blocks:
            wa = mixed(rs, slice(3 * rw, 3 * rw + W_LORA + A_LORA))
            wa_b.append(jnp.where(lane < W_LORA, jnp.tanh(wa), wa).astype(BF16))
            gd = mixed(rs, slice(3 * rw + W_LORA + A_LORA, RWKV_SHIFT_COLS))
            gd_b.append(jax.nn.sigmoid(gd).astype(BF16))
            yield
        lora = _dot(cat(wa_b), wwa_ref[...])
        g = _dot(cat(gd_b), g2_ref[...])
        yield
        r_b, k_b, v_b = [], [], []
        for rs in blocks:
            r_b.append(mixed(rs, slice(lo, lo + hw)))
            k_b.append(mixed(rs, slice(rw + lo, rw + lo + hw)))
            v_b.append(mixed(rs, slice(2 * rw + lo, 2 * rw + lo + hw)))
            yield
        logw_b, a_b, kk_b, sq_b = [], [], [], []
        for b, rs in enumerate(blocks):
            w_log = -jax.nn.softplus(-(w0_ref[:, ch] + lora[rs, ch])) - 0.5
            logw_b.append(-jnp.exp(w_log))
            a_b.append(jax.nn.sigmoid(a0_ref[:, ch] + lora[rs, rw + lo:rw + lo + hw]))
            kk = k_b[b] * kk_ref[:, ch]
            kk_b.append(kk)
            sq_b.append((kk * kk).astype(BF16))
            yield
        kk_ss = _half_dot(cat(sq_b), hones_ref)
        cum = _dot_split_rhs(incl_bf, cat(logw_b), 2)
        yield
        bon_b, rt_b, zt_b, kt_b, bt_b, kh_b, bh_b, pt_b = [], [], [], [], [], [], [], []
        for b, rs in enumerate(blocks):
            c_end = (rs.start // lc + 1) * lc
            cm = cum[rs]
            tot = cum[c_end - 1:c_end, :]
            a = a_b[b]
            kk = kk_b[b] * jnp.minimum(lax.rsqrt(kk_ss[rs]), 1e12)
            k2 = k_b[b] * (1.0 + (a - 1.0) * ka_ref[:, ch])
            bon_b.append((r_b[b] * k2 * rk_ref[:, ch]).astype(BF16))
            bv = kk * a
            inv_p = jnp.exp(-cm)
            to_end = jnp.exp(tot - cm)
            rt_b.append(r_b[b] * jnp.exp(cm))
            zt_b.append(-kk * jnp.exp(cm - logw_b[b]))
            kt_b.append(k2 * inv_p)
            bt_b.append(bv * inv_p)
            kh_b.append(k2 * to_end)
            bh_b.append(bv * to_end)
            pt_b.append(jnp.broadcast_to(jnp.exp(tot), (ROW_BLOCK, hw)))
            yield
        v = cat(v_b)
        ops = dict(v=v, g=g, bonus=_half_dot(cat(bon_b), hones_ref) * v,
                   rt=cat(rt_b), zt=cat(zt_b), kt=cat(kt_b), bt=cat(bt_b))
        yield
        ops.update(kh_t=cat(kh_b).T, bh_t=cat(bh_b).T, pt_t=cat(pt_b).T)
        yield
        return ops

    def side_by_side(m):
        out = m[0:lc]
        for c in range(1, nchunk):
            out = out + m[c * lc:(c + 1) * lc]
        return out

    def block_diag(m):
        return jnp.where(same, jnp.concatenate([m] * nchunk, axis=0), 0.0)

    def wkv(half, ops):
        heads = range(hph)
        hsl = [slice(h * hd, (h + 1) * hd) for h in heads]
        z_h = [ops['zt'][:, s] for s in hsl]
        r_h = [ops['rt'][:, s] for s in hsl]
        v_h = [ops['v'][:, s] for s in hsl]
        kh_t, bh_t, pt_t = ops['kh_t'], ops['bh_t'], ops['pt_t']
        n_zb, a_zk, a_rb, a_rk = [], [], [], []
        for h in heads:
            zr = jnp.concatenate([z_h[h], r_h[h]], axis=0)
            bk = jnp.concatenate([ops['bt'][:, hsl[h]], ops['kt'][:, hsl[h]]], axis=0)
            aa = _dot_nt(zr, bk)
            n_zb.append(jnp.where(strict, aa[0:tm, 0:tm], 0.0))
            a_zk.append(jnp.where(strict, aa[0:tm, tm:2 * tm], 0.0))
            a_rb.append(jnp.where(incl, aa[tm:2 * tm, 0:tm], 0.0))
            a_rk.append(jnp.where(incl, aa[tm:2 * tm, tm:2 * tm], 0.0))
            yield
        xs = [side_by_side(n) for n in n_zb]
        ts = [eye_sbs + x for x in xs]
        for h in heads:
            xs[h] = _dot(xs[h], block_diag(xs[h]))
            yield
        span = 4
        while span < lc:
            for h in heads:
                both = _dot(jnp.concatenate([ts[h], xs[h]], axis=0), block_diag(xs[h]))
                ts[h] = ts[h] + both[0:lc]
                xs[h] = both[lc:2 * lc]
                yield
            span *= 2
        t_inv = [block_diag(ts[h] + _dot(ts[h], block_diag(xs[h]))) for h in heads]
        yield
        wv = [_dot(a_zk[h], v_h[h]) for h in heads]
        yield
        tzu = [_dot(t_inv[h], jnp.concatenate([z_h[h], wv[h]], axis=1)) for h in heads]
        yield
        ar = [_dot(a_rb[h], tzu[h]) for h in heads]
        m_y = [r_h[h] + ar[h][:, 0:hd] for h in heads]
        yield
        y1 = [ar[h][:, hd:2 * hd] + _dot(a_rk[h], v_h[h]) for h in heads]
        yield
        phi, gam = [], []
        for h in heads:
            b_blk = jnp.where(same, jnp.concatenate([bh_t[hsl[h], :]] * nchunk, axis=0), 0.0)
            k_blk = jnp.where(same, jnp.concatenate([kh_t[hsl[h], :]] * nchunk, axis=0), 0.0)
            pg = _dot(b_blk, tzu[h])
            phi.append(pg[:, 0:hd])
            gam.append(pg[:, hd:2 * hd] + _dot(k_blk, v_h[h]))
            yield
        st = [state_s[t, half * hph + h] for h in heads]
        y_rows = [[] for _ in heads]
        for c in range(nchunk):
            cs = slice(c * lc, (c + 1) * lc)
            for h in heads:
                y_rows[h].append(_dot(m_y[h][cs], st[h]) + y1[h][cs])
                decay = pt_t[hsl[h], c * lc:c * lc + 1]
                st[h] = st[h] * decay + _dot(phi[h][cs], st[h]) + gam[h][cs]
            yield
        for h in heads:
            state_s[t, half * hph + h] = st[h]
        return jnp.concatenate([jnp.concatenate(rows, axis=0) for rows in y_rows], axis=1)

    def finish(half, y, ops):
        ch = slice(half * hw, (half + 1) * hw)
        mean = _half_dot(y, hones_ref) * (1.0 / hd)
        yc = y - mean
        var = _half_dot(yc * yc, hones_ref) * (1.0 / hd)
        yield
        y = yc * lax.rsqrt(var + GN_EPS) * lnw_ref[:, ch] + lnb_ref[:, ch]
        return (y + ops['bonus']) * ops['g'][:, ch]

    def lru():
        c0 = RWKV_SHIFT_COLS
        xc_b = []
        for rs in blocks:
            xc = cb_ref[...]
            for j in range(LRU_CONV):
                xc = xc + cw_ref[j:j + 1, :] * lagged(rs, slice(c0, c0 + LRU_WIDTH), LRU_CONV - 1 - j)
            xc_b.append(xc)
            yield
        xc = cat(xc_b)
        gate_a = _half_dot(xc, gaw_ref) + gab_ref[...]
        gate_x = _half_dot(xc, gxw_ref) + gxb_ref[...]
        yield
        neg_c_softplus = -LRU_C * jax.nn.softplus(-lam_ref[...])
        in_group = lax.broadcasted_iota(jnp.int32, (ROW_BLOCK, LRU_WIDTH), 0) % SUBLANES
        parts = []
        for b, rs in enumerate(blocks):
            log_a = neg_c_softplus * jax.nn.sigmoid(gate_a[rs])
            sa = jnp.exp(log_a)
            su = xc_b[b] * jax.nn.sigmoid(gate_x[rs]) * jnp.sqrt(1.0 - jnp.exp(2.0 * log_a))
            for d in (1, 2, 4):
                keep = in_group >= d
                su, sa = (su + sa * jnp.where(keep, pltpu.roll(su, d, 0), 0.0),
                          sa * jnp.where(keep, pltpu.roll(sa, d, 0), 1.0))
            gate_b = jax.nn.gelu(current(rs, slice(c0 + LRU_WIDTH, c0 + 2 * LRU_WIDTH)))
            parts.append((su, sa, gate_b))
            yield
        h_prev = hcar_s[t, 0:1, :]
        h_groups = []
        for su, sa, gate_b in parts:
            for gidx in range(ROW_BLOCK // SUBLANES):
                gs = slice(gidx * SUBLANES, (gidx + 1) * SUBLANES)
                hg = su[gs] + sa[gs] * h_prev
                h_groups.append(hg * gate_b[gs])
                h_prev = hg[SUBLANES - 1:SUBLANES, :]
            yield
        hcar_s[t] = jnp.broadcast_to(h_prev, hcar_s.shape[1:])
        return jnp.concatenate(h_groups, axis=0)

    def output(y_a, y_b):
        y_cat = jnp.concatenate([y_a, y_b], axis=1).astype(BF16)
        return x + jnp.dot(y_cat, wout_ref[...], preferred_element_type=F32)

    return dict(inproj=inproj, operands=lambda: operands(0), wkv=lambda ops: wkv(0, ops), lru=lru,
                finish=lambda y, ops: finish(0, y, ops), output=output)


def _even_body(x_ref, *refs, tm):
    o_ref, p_s, state_s, hcar_s = refs[-4:]
    i = pl.program_id(1)

    @pl.when(i == 0)
    def _():
        state_s[...] = jnp.zeros_like(state_s)
        hcar_s[...] = jnp.zeros_like(hcar_s)
        p_s[...] = jnp.zeros_like(p_s)

    t0, t1 = [_even_tile(t, x_ref, *refs[:-4], p_s, state_s, hcar_s, tm=tm) for t in range(2)]
    _run(t0['inproj']())
    ops0, _ = _run_together(t0['operands'](), t1['inproj']())
    y0, ops1 = _run_together(t0['wkv'](ops0), t1['operands']())
    y1, (yb0, ya0, yb1) = _run_together(
        t1['wkv'](ops1), _chain(t0['lru'](), t0['finish'](y0, ops0), t1['lru']()))
    o_ref[0] = t0['output'](ya0, yb0)
    ya1 = _run(t1['finish'](y1, ops1))
    o_ref[1] = t1['output'](ya1, yb1)


def _block_diag(w):
    h, n, _ = w.shape
    eye = jnp.eye(h, dtype=w.dtype)
    return (eye[:, None, :, None] * w[:, :, None, :]).reshape(h * n, h * n)


def _row(v):
    return v.reshape(1, -1).astype(F32)


def _const_spec(shape):
    nd = len(shape)
    return pl.BlockSpec(shape, lambda *_: (0,) * nd, pipeline_mode=pl.Buffered(1))


class _Layered:
    def __init__(self, stacked, layer):
        if stacked.ndim == 2:
            stacked = stacked.reshape(stacked.shape[0], 1, stacked.shape[1])
        self.stacked, self.layer = stacked, layer


def _operand(c):
    return c.stacked if isinstance(c, _Layered) else c


def _operand_spec(c):
    if not isinstance(c, _Layered):
        return _const_spec(c.shape)
    rest, layer = c.stacked.shape[1:], c.layer
    return pl.BlockSpec((None,) + rest, lambda *_: (layer,) + (0,) * len(rest),
                        pipeline_mode=pl.Buffered(1))


def _even_layer(x2, bsz, seq, j, norm_g, w_in, mu, w0, w2, a0, a2, g2, k_k, k_a, r_k, ln_w, ln_b,
                conv_w, conv_b, gate_a_w, gate_a_b, gate_x_w, gate_x_b, lru_lambda, w_out):
    tm = TM_MIX
    nt = seq // tm
    rw = RWKV_WIDTH
    wwa = jnp.zeros((W_LORA + A_LORA, 2 * rw), F32)
    wwa = wwa.at[:W_LORA, :rw].set(w2[j]).at[W_LORA:, rw:].set(a2[j]).astype(BF16)

    def halves(w):
        hh = w.shape[0] // 2
        return jnp.stack([_block_diag(w[:hh]), _block_diag(w[hh:])]).astype(BF16)

    def lay(stacked):
        return _Layered(stacked.astype(F32), j)

    hones = halves(jnp.ones((RWKV_HEADS, RWKV_HEAD, RWKV_HEAD), F32))
    consts = [
        lay(norm_g), w_in[j].astype(BF16), lay(mu), lay(w0), wwa, lay(a0), g2[j].astype(BF16),
        lay(k_k), lay(k_a), _row(r_k[j]), lay(ln_w), lay(ln_b), lay(conv_w), lay(conv_b),
        halves(gate_a_w[j]), lay(gate_a_b), halves(gate_x_w[j]),
        lay(gate_x_b), lay(lru_lambda), hones, w_out[j].astype(BF16),
    ]
    assert bsz % 2 == 0
    x3 = x2.reshape(2, (bsz // 2) * seq, D_MODEL)
    xspec = pl.BlockSpec((2, tm, D_MODEL), lambda b, i: (0, b * nt + i, 0))
    out = pl.pallas_call(
        functools.partial(_even_body, tm=tm),
        grid=(bsz // 2, nt),
        in_specs=[xspec] + [_operand_spec(c) for c in consts],
        out_specs=xspec,
        out_shape=jax.ShapeDtypeStruct(x3.shape, F32),
        scratch_shapes=[
            pltpu.VMEM((2, SUBLANES, EVEN_IN_COLS), F32),
            pltpu.VMEM((2, RWKV_HEADS, RWKV_HEAD, RWKV_HEAD), F32),
            pltpu.VMEM((2, SUBLANES, LRU_WIDTH), F32),
        ],
        compiler_params=pltpu.CompilerParams(
            dimension_semantics=("arbitrary", "arbitrary"), vmem_limit_bytes=VMEM_LIMIT),
        name="even_mixer",
    )(x3, *[_operand(c) for c in consts])
    return out.reshape(x2.shape)


def _s5_param_body(are_ref, aim_ref, ldt_ref, bre_ref, bim_ref, abre_ref, abim_ref, bbre_ref, bbim_ref):
    lam_re = jnp.minimum(are_ref[...], -1e-4)
    lam_im = aim_ref[...]
    dt = jnp.exp(ldt_ref[...])
    mag = jnp.exp(lam_re * dt)
    ab_re = mag * jnp.cos(lam_im * dt)
    ab_im = mag * jnp.sin(lam_im * dt)
    den = lam_re * lam_re + lam_im * lam_im
    zr = ab_re - 1.0
    q_re = (zr * lam_re + ab_im * lam_im) / den
    q_im = (ab_im * lam_re - zr * lam_im) / den
    abre_ref[...] = ab_re
    abim_ref[...] = ab_im
    for c in range(S5_GROUP):
        bbre_ref[c] = q_re * bre_ref[c] - q_im * bim_ref[c]
        bbim_ref[c] = q_re * bim_ref[c] + q_im * bre_ref[c]


def _s5_params(a_re, a_im, log_dt, b_re, b_im):
    g, n = S5_GROUPS, S5_STATE
    shp = jax.ShapeDtypeStruct
    return pl.pallas_call(
        _s5_param_body,
        out_shape=(shp((g, n), F32), shp((g, n), F32),
                   shp((S5_GROUP, g, n), F32), shp((S5_GROUP, g, n), F32)),
        name="s5_params",
    )(a_re.astype(F32), a_im.astype(F32), log_dt.reshape(g, 1).astype(F32),
      jnp.transpose(b_re, (2, 0, 1)).astype(F32), jnp.transpose(b_im, (2, 0, 1)).astype(F32))


def _odd_body(x_ref, perm_ref, permt_ref, ng_ref, win_ref, bre_ref, bim_ref, cre_ref, cim_ref,
              are_ref, aim_ref, d_ref, wglu_ref, o_ref, sre_s, sim_s, car_s, *, tt):
    i = pl.program_id(1)
    rows = SUBLANES * tt
    nstrip = sre_s.shape[0]
    nblk = bre_ref.shape[0]
    per = nstrip // nblk
    cw = bre_ref.shape[1]

    @pl.when(i == 0)
    def _():
        car_s[...] = jnp.zeros_like(car_s)

    x = x_ref[...].reshape(rows, D_MODEL)
    xn = _rms(x, ng_ref[...]).astype(BF16)
    xn = jnp.dot(perm_ref[...], xn, preferred_element_type=F32).astype(BF16)
    u = jnp.dot(xn, win_ref[...], preferred_element_type=F32)
    ub = u.astype(BF16)
    for j in range(nblk):
        uj = ub[:, j * cw:(j + 1) * cw]
        pr = jnp.dot(uj, bre_ref[j], preferred_element_type=F32)
        pi = jnp.dot(uj, bim_ref[j], preferred_element_type=F32)
        for q in range(per):
            sre_s[j * per + q] = pr[:, q * S5_LANES:(q + 1) * S5_LANES]
            sim_s[j * per + q] = pi[:, q * S5_LANES:(q + 1) * S5_LANES]

    for s in range(nstrip):
        ar = are_ref[s]
        ai = aim_ref[s]
        xr = car_s[0, s]
        xi = car_s[1, s]
        for t in range(tt):
            rs = slice(t * SUBLANES, (t + 1) * SUBLANES)
            xr, xi = (ar * xr - ai * xi + sre_s[s, rs, :], ar * xi + ai * xr + sim_s[s, rs, :])
            sre_s[s, rs, :] = xr
            sim_s[s, rs, :] = xi
        car_s[0, s] = xr
        car_s[1, s] = xi

    ys = []
    for j in range(nblk):
        acc = None
        for q in range(per):
            rsl = slice(q * S5_LANES, (q + 1) * S5_LANES)
            t = (_dot(sre_s[j * per + q], cre_ref[j, rsl, :])
                 - _dot(sim_s[j * per + q], cim_ref[j, rsl, :]))
            acc = t if acc is None else acc + t
        ys.append(acc)
    y = jnp.concatenate(ys, axis=1)
    yy = jax.nn.gelu(y + d_ref[...] * u).astype(BF16)
    yy = jnp.dot(permt_ref[...], yy, preferred_element_type=F32).astype(BF16)
    og = jnp.dot(yy, wglu_ref[...], preferred_element_type=F32)
    out = x + og[:, 0:D_MODEL] * jax.nn.sigmoid(og[:, D_MODEL:2 * D_MODEL])
    o_ref[...] = out.reshape(o_ref.shape)


def _odd_layer(x2, bsz, seq, norm_g, w_in, a_re, a_im, log_dt, b_re, b_im, c_re, c_im, d_skip, w_glu):
    tt = TT_ODD
    assert bsz % SUBLANES == 0 and seq % tt == 0
    nt = seq // tt
    rows = SUBLANES * tt
    g, n, gc = S5_GROUPS, S5_STATE, S5_GROUP
    gpb = 8
    nblk = g // gpb
    nstrip = g * n // S5_LANES
    ab_re, ab_im, bb_re, bb_im = _s5_params(a_re, a_im, log_dt, b_re, b_im)
    eye = jnp.eye(gpb, dtype=F32)

    def b_blocks(bb):
        t = bb.reshape(gc, nblk, gpb, n)
        t = jnp.einsum('cjgn,gh->jgchn', t, eye)
        return t.reshape(nblk, gpb * gc, gpb * n).astype(BF16)

    def c_blocks(cc):
        t = cc.astype(F32).reshape(nblk, gpb, gc, n)
        t = jnp.einsum('jgcn,gh->jgnhc', t, eye)
        return t.reshape(nblk, gpb * n, gpb * gc).astype(BF16)

    def strips(p):
        return jnp.broadcast_to(p.reshape(nstrip, 1, S5_LANES), (nstrip, SUBLANES, S5_LANES))

    dst = jnp.arange(rows)
    src = (dst % SUBLANES) * tt + dst // SUBLANES
    perm = (src[:, None] == jnp.arange(rows)[None, :]).astype(BF16)

    consts = [perm, perm.T, norm_g, w_in.astype(BF16), b_blocks(bb_re), b_blocks(bb_im),
              c_blocks(c_re), c_blocks(c_im), strips(ab_re), strips(ab_im), d_skip,
              w_glu.astype(BF16)]
    x4 = x2.reshape(bsz, nt, tt, D_MODEL)
    xspec = pl.BlockSpec((SUBLANES, 1, tt, D_MODEL), lambda b, i: (b, i, 0, 0))
    out = pl.pallas_call(
        functools.partial(_odd_body, tt=tt),
        grid=(bsz // SUBLANES, nt),
        in_specs=[xspec] + [_operand_spec(c) for c in consts],
        out_specs=xspec,
        out_shape=jax.ShapeDtypeStruct(x4.shape, F32),
        scratch_shapes=[
            pltpu.VMEM((nstrip, rows, S5_LANES), F32),
            pltpu.VMEM((nstrip, rows, S5_LANES), F32),
            pltpu.VMEM((2, nstrip, SUBLANES, S5_LANES), F32),
        ],
        compiler_params=pltpu.CompilerParams(
            dimension_semantics=("arbitrary", "arbitrary"), vmem_limit_bytes=VMEM_LIMIT),
        name="odd_mixer",
    )(x4, *[_operand(c) for c in consts])
    return out.reshape(x2.shape)


def _ffn_body(x_ref, ng_ref, wup_ref, cw_ref, cb_ref, wd_ref, fg_ref, o_ref,
              h_s, act_s, *, tm, tiles_per_seq, final_norm):
    i = pl.program_id(0)
    tf = TF_FFN
    x = x_ref[...]
    xn = _rms(x, ng_ref[...]).astype(BF16)

    @pl.when(i % tiles_per_seq == 0)
    def _():
        h_s[...] = jnp.zeros_like(h_s)

    def conv(c0):
        h = jnp.dot(xn, wup_ref[:, c0:c0 + tf], preferred_element_type=F32)
        full = jnp.concatenate([h_s[:, c0:c0 + tf], h], axis=0)
        h_s[:, c0:c0 + tf] = h[tm - SUBLANES:tm, :]
        out = cb_ref[:, c0:c0 + tf] + cw_ref[FFN_CONV - 1:FFN_CONV, c0:c0 + tf] * h
        for t in range(FFN_CONV - 1):
            lag = FFN_CONV - 1 - t
            out = out + cw_ref[t:t + 1, c0:c0 + tf] * full[SUBLANES - lag:SUBLANES - lag + tm, :]
        return out

    for c in range(D_FF // tf):
        act = jax.nn.silu(conv(c * tf)) * conv(D_FF + c * tf)
        act_s[:, c * tf:(c + 1) * tf] = act.astype(BF16)
    y = x + jnp.dot(act_s[...], wd_ref[...], preferred_element_type=F32)
    if final_norm:
        y = _rms(y, fg_ref[...])
    o_ref[...] = y


def _ffn_layer(x2, seq, layer, norm_g, w_up_all, conv_w, conv_b, w_down_all, final_g, final_norm):
    tm = TM_FFN
    m = x2.shape[0]
    consts = [_Layered(norm_g.astype(F32), layer), _Layered(w_up_all, layer),
              _Layered(conv_w.astype(F32), layer), _Layered(conv_b.astype(F32), layer),
              _Layered(w_down_all, layer), _row(final_g)]
    in_specs = [pl.BlockSpec((tm, D_MODEL), lambda i: (i, 0))]
    in_specs += [_operand_spec(c) for c in consts]
    return pl.pallas_call(
        functools.partial(_ffn_body, tm=tm, tiles_per_seq=seq // tm, final_norm=final_norm),
        grid=(m // tm,),
        in_specs=in_specs,
        out_specs=pl.BlockSpec((tm, D_MODEL), lambda i: (i, 0)),
        out_shape=jax.ShapeDtypeStruct(x2.shape, F32),
        scratch_shapes=[
            pltpu.VMEM((SUBLANES, 2 * D_FF), F32),
            pltpu.VMEM((tm, D_FF), BF16),
        ],
        compiler_params=pltpu.CompilerParams(
            dimension_semantics=("arbitrary",), vmem_limit_bytes=VMEM_LIMIT),
        name="conv_ffn",
    )(x2, *[_operand(c) for c in consts])


def kernel(x, e_norm_g, e_w_in, e_mu, e_w0, e_w2, e_a0, e_a2, e_g2, e_k_k, e_k_a, e_r_k,
           e_ln_w, e_ln_b, e_conv_w, e_conv_b, e_gate_a_w, e_gate_a_b, e_gate_x_w, e_gate_x_b,
           e_lru_lambda, e_w_out, o_norm_g, o_w_in, o_A_re, o_A_im, o_log_dt, o_B_re, o_B_im,
           o_C_re, o_C_im, o_D, o_w_glu, f_norm_g, f_w_up, f_conv_w, f_conv_b, f_w_down,
           final_norm_g):
    bsz, seq, dm = x.shape
    assert dm == D_MODEL and seq % TM_FFN == 0 and seq % TM_MIX == 0
    depth = f_norm_g.shape[0]
    h = x.reshape(bsz * seq, dm).astype(F32)
    f_w_up_bf = f_w_up.astype(BF16)
    f_w_down_bf = f_w_down.astype(BF16)
    for i in range(depth):
        j = i // 2
        if i % 2 == 0:
            h = _even_layer(h, bsz, seq, j, e_norm_g, e_w_in, e_mu, e_w0, e_w2, e_a0, e_a2, e_g2,
                            e_k_k, e_k_a, e_r_k, e_ln_w, e_ln_b, e_conv_w, e_conv_b, e_gate_a_w,
                            e_gate_a_b, e_gate_x_w, e_gate_x_b, e_lru_lambda, e_w_out)
        else:
            h = _odd_layer(h, bsz, seq, _Layered(o_norm_g.astype(F32), j), o_w_in[j], o_A_re[j],
                           o_A_im[j], o_log_dt[j], o_B_re[j], o_B_im[j], o_C_re[j], o_C_im[j],
                           _Layered(o_D.astype(F32), j), o_w_glu[j])
        h = _ffn_layer(h, seq, i, f_norm_g, f_w_up_bf, f_conv_w, f_conv_b, f_w_down_bf,
                       final_norm_g, final_norm=(i == depth - 1))
    return h.reshape(bsz, seq, dm).astype(x.dtype)
```

```python
import functools

import jax
import jax.numpy as jnp
from jax import lax
from jax.experimental import pallas as pl
from jax.experimental.pallas import tpu as pltpu

F32 = jnp.float32
BF16 = jnp.bfloat16

D_MODEL = 1024
RWKV_WIDTH = 512
RWKV_HEAD = 64
RWKV_HEADS = 8
W_LORA = 64
A_LORA = 64
G_LORA = 128
LRU_WIDTH = 512
LRU_BLOCKS = 8
LRU_BLOCK = 64
LRU_CONV = 4
LRU_C = 8.0
S5_WIDTH = 1024
S5_GROUP = 16
S5_GROUPS = 64
S5_STATE = 64
D_FF = 2816
FFN_CONV = 3
NORM_EPS = 1e-6
GN_EPS = 64e-5
RWKV_SHIFT_COLS = 3 * RWKV_WIDTH + W_LORA + A_LORA + G_LORA
EVEN_IN_COLS = RWKV_SHIFT_COLS + 2 * LRU_WIDTH

SUBLANES = 8
TM_MIX = 256
WKV_CHUNK = 64
ROW_BLOCK = 64
TT_ODD = 32
ODD_SUBTILES = 2
S5_LANES = 256
TM_FFN = 1024
TF_FFN = 256
VMEM_LIMIT = 56 * 1024 * 1024


def _dot(a, b):
    return jnp.dot(a.astype(BF16), b.astype(BF16), preferred_element_type=F32)


def _dot_nt(a, b):
    return lax.dot_general(a.astype(BF16), b.astype(BF16), (((1,), (1,)), ((), ())),
                           preferred_element_type=F32)


def _split(x, n):
    parts = []
    for _ in range(n - 1):
        h = x.astype(BF16)
        parts.append(h)
        x = x - h.astype(F32)
    parts.append(x.astype(BF16))
    return parts


def _dot_split_rhs(a_bf16, b, n):
    acc = None
    for piece in _split(b, n):
        t = jnp.dot(a_bf16, piece, preferred_element_type=F32)
        acc = t if acc is None else acc + t
    return acc


def _rms(x, g):
    ms = jnp.mean(x * x, axis=-1, keepdims=True)
    return x * lax.rsqrt(ms + NORM_EPS) * g


def _half_dot(x, w_ref):
    half = x.shape[1] // 2
    return jnp.concatenate([_dot(x[:, 0:half], w_ref[0]), _dot(x[:, half:2 * half], w_ref[1])], axis=1)


def _run(stream):
    while True:
        try:
            next(stream)
        except StopIteration as stop:
            return stop.value


def _run_together(main, side):
    results, done = [None, None], [False, False]
    streams = (main, side)
    while not all(done):
        for idx, stream in enumerate(streams):
            if not done[idx]:
                try:
                    next(stream)
                except StopIteration as stop:
                    results[idx], done[idx] = stop.value, True
    return results


def _chain(*streams):
    out = []
    for stream in streams:
        out.append((yield from stream))
    return out


def _even_tile(t, x_ref, ng_ref, win_ref, mu_ref, w0_ref, wwa_ref, a0_ref, g2_ref,
               kk_ref, ka_ref, rk_ref, lnw_ref, lnb_ref, cw_ref, cb_ref, gaw_ref, gab_ref,
               gxw_ref, gxb_ref, lam_ref, hones_ref, wout_ref, p_s, state_s, hcar_s, *, tm):
    rw = RWKV_WIDTH
    hw = rw
    hd = RWKV_HEAD
    hph = RWKV_HEADS
    lc = WKV_CHUNK
    nchunk = tm // lc

    x = x_ref[t]
    xn = _rms(x, ng_ref[...]).astype(BF16)
    bounds = (0, rw, 2 * rw, 3 * rw, RWKV_SHIFT_COLS, RWKV_SHIFT_COLS + LRU_WIDTH, EVEN_IN_COLS)
    p_parts, full_parts = [], []

    def inproj():
        for lo, hi in zip(bounds[:-1], bounds[1:]):
            part = jnp.dot(xn, win_ref[:, lo:hi], preferred_element_type=F32)
            full_parts.append(jnp.concatenate([p_s[t, :, lo:hi], part], axis=0))
            p_s[t, :, lo:hi] = part[tm - SUBLANES:tm, :]
            p_parts.append(part)
            yield

    def locate(cs):
        idx = max(n for n in range(len(bounds) - 1) if bounds[n] <= cs.start)
        assert cs.stop <= bounds[idx + 1]
        return idx, slice(cs.start - bounds[idx], cs.stop - bounds[idx])

    def current(rs, cs):
        idx, loc = locate(cs)
        return p_parts[idx][rs, loc]

    def lagged(rs, cs, lag):
        idx, loc = locate(cs)
        return full_parts[idx][SUBLANES - lag + rs.start:SUBLANES - lag + rs.stop, loc]

    def mixed(rs, cs):
        pa = current(rs, cs)
        return pa + mu_ref[:, cs] * (lagged(rs, cs, 1) - pa)

    blocks = [slice(b * ROW_BLOCK, (b + 1) * ROW_BLOCK) for b in range(tm // ROW_BLOCK)]
    cat = lambda parts: jnp.concatenate(parts, axis=0)
    row = lax.broadcasted_iota(jnp.int32, (tm, tm), 0)
    col = lax.broadcasted_iota(jnp.int32, (tm, tm), 1)
    same = (row // lc) == (col // lc)
    incl = same & (row >= col)
    strict = same & (row > col)
    incl_bf = incl.astype(BF16)
    eye_sbs = (lax.broadcasted_iota(jnp.int32, (lc, tm), 0)
               == lax.broadcasted_iota(jnp.int32, (lc, tm), 1) % lc).astype(F32)

    def operands(half):
        lo = half * hw
        ch = slice(lo, lo + hw)
        lane = lax.broadcasted_iota(jnp.int32, (ROW_BLOCK, W_LORA + A_LORA), 1)
        wa_b, gd_b = [], []
        for rs in blocks:
            wa = mixed(rs, slice(3 * rw, 3 * rw + W_LORA + A_LORA))
            wa_b.append(jnp.where(lane < W_LORA, jnp.tanh(wa), wa).astype(BF16))
            gd = mixed(rs, slice(3 * rw + W_LORA + A_LORA, RWKV_SHIFT_COLS))
            gd_b.append(jax.nn.sigmoid(gd).astype(BF16))
            yield
        lora = _dot(cat(wa_b), wwa_ref[...])
        g = _dot(cat(gd_b), g2_ref[...])
        yield
        r_b, k_b, v_b = [], [], []
        for rs in blocks:
            r_b.append(mixed(rs, slice(lo, lo + hw)))
            k_b.append(mixed(rs, slice(rw + lo, rw + lo + hw)))
            v_b.append(mixed(rs, slice(2 * rw + lo, 2 * rw + lo + hw)))
            yield
        logw_b, a_b, kk_b, sq_b = [], [], [], []
        for b, rs in enumerate(blocks):
            w_log = -jax.nn.softplus(-(w0_ref[:, ch] + lora[rs, ch])) - 0.5
            logw_b.append(-jnp.exp(w_log))
            a_b.append(jax.nn.sigmoid(a0_ref[:, ch] + lora[rs, rw + lo:rw + lo + hw]))
            kk = k_b[b] * kk_ref[:, ch]
            kk_b.append(kk)
            sq_b.append((kk * kk).astype(BF16))
            yield
        kk_ss = _half_dot(cat(sq_b), hones_ref)
        cum = _dot_split_rhs(incl_bf, cat(logw_b), 2)
        yield
        bon_b, rt_b, zt_b, kt_b, bt_b, kh_b, bh_b, pt_b = [], [], [], [], [], [], [], []
        for b, rs in enumerate(blocks):
            c_end = (rs.start // lc + 1) * lc
            cm = cum[rs]
            tot = cum[c_end - 1:c_end, :]
            a = a_b[b]
            kk = kk_b[b] * jnp.minimum(lax.rsqrt(kk_ss[rs]), 1e12)
            k2 = k_b[b] * (1.0 + (a - 1.0) * ka_ref[:, ch])
            bon_b.append((r_b[b] * k2 * rk_ref[:, ch]).astype(BF16))
            bv = kk * a
            inv_p = jnp.exp(-cm)
            to_end = jnp.exp(tot - cm)
            rt_b.append(r_b[b] * jnp.exp(cm))
            zt_b.append(-kk * jnp.exp(cm - logw_b[b]))
            kt_b.append(k2 * inv_p)
            bt_b.append(bv * inv_p)
            kh_b.append(k2 * to_end)
            bh_b.append(bv * to_end)
            pt_b.append(jnp.broadcast_to(jnp.exp(tot), (ROW_BLOCK, hw)))
            yield
        v = cat(v_b)
        ops = dict(v=v, g=g, bonus=_half_dot(cat(bon_b), hones_ref) * v,
                   rt=cat(rt_b), zt=cat(zt_b), kt=cat(kt_b), bt=cat(bt_b))
        yield
        ops.update(kh_t=cat(kh_b).T, bh_t=cat(bh_b).T, pt_t=cat(pt_b).T)
        yield
        return ops

    def side_by_side(m):
        out = m[0:lc]
        for c in range(1, nchunk):
            out = out + m[c * lc:(c + 1) * lc]
        return out

    def block_diag(m):
        return jnp.where(same, jnp.concatenate([m] * nchunk, axis=0), 0.0)

    def wkv(half, ops):
        heads = range(hph)
        hsl = [slice(h * hd, (h + 1) * hd) for h in heads]
        z_h = [ops['zt'][:, s] for s in hsl]
        r_h = [ops['rt'][:, s] for s in hsl]
        v_h = [ops['v'][:, s] for s in hsl]
        kh_t, bh_t, pt_t = ops['kh_t'], ops['bh_t'], ops['pt_t']
        n_zb, a_zk, a_rb, a_rk = [], [], [], []
        for h in heads:
            zr = jnp.concatenate([z_h[h], r_h[h]], axis=0)
            bk = jnp.concatenate([ops['bt'][:, hsl[h]], ops['kt'][:, hsl[h]]], axis=0)
            aa = _dot_nt(zr, bk)
            n_zb.append(jnp.where(strict, aa[0:tm, 0:tm], 0.0))
            a_zk.append(jnp.where(strict, aa[0:tm, tm:2 * tm], 0.0))
            a_rb.append(jnp.where(incl, aa[tm:2 * tm, 0:tm], 0.0))
            a_rk.append(jnp.where(incl, aa[tm:2 * tm, tm:2 * tm], 0.0))
            yield
        xs = [side_by_side(n) for n in n_zb]
        ts = [eye_sbs + x for x in xs]
        for h in heads:
            xs[h] = _dot(xs[h], block_diag(xs[h]))
            yield
        span = 4
        while span < lc:
            for h in heads:
                both = _dot(jnp.concatenate([ts[h], xs[h]], axis=0), block_diag(xs[h]))
                ts[h] = ts[h] + both[0:lc]
                xs[h] = both[lc:2 * lc]
                yield
            span *= 2
        t_inv = [block_diag(ts[h] + _dot(ts[h], block_diag(xs[h]))) for h in heads]
        yield
        wv = [_dot(a_zk[h], v_h[h]) for h in heads]
        yield
        tzu = [_dot(t_inv[h], jnp.concatenate([z_h[h], wv[h]], axis=1)) for h in heads]
        yield
        ar = [_dot(a_rb[h], tzu[h]) for h in heads]
        m_y = [r_h[h] + ar[h][:, 0:hd] for h in heads]
        yield
        y1 = [ar[h][:, hd:2 * hd] + _dot(a_rk[h], v_h[h]) for h in heads]
        yield
        phi, gam = [], []
        for h in heads:
            b_blk = jnp.where(same, jnp.concatenate([bh_t[hsl[h], :]] * nchunk, axis=0), 0.0)
            k_blk = jnp.where(same, jnp.concatenate([kh_t[hsl[h], :]] * nchunk, axis=0), 0.0)
            pg = _dot(b_blk, tzu[h])
            phi.append(pg[:, 0:hd])
            gam.append(pg[:, hd:2 * hd] + _dot(k_blk, v_h[h]))
            yield
        st = [state_s[t, half * hph + h] for h in heads]
        y_rows = [[] for _ in heads]
        for c in range(nchunk):
            cs = slice(c * lc, (c + 1) * lc)
            for h in heads:
                y_rows[h].append(_dot(m_y[h][cs], st[h]) + y1[h][cs])
                decay = pt_t[hsl[h], c * lc:c * lc + 1]
                st[h] = st[h] * decay + _dot(phi[h][cs], st[h]) + gam[h][cs]
            yield
        for h in heads:
            state_s[t, half * hph + h] = st[h]
        return jnp.concatenate([jnp.concatenate(rows, axis=0) for rows in y_rows], axis=1)

    def finish(half, y, ops):
        ch = slice(half * hw, (half + 1) * hw)
        mean = _half_dot(y, hones_ref) * (1.0 / hd)
        yc = y - mean
        var = _half_dot(yc * yc, hones_ref) * (1.0 / hd)
        yield
        y = yc * lax.rsqrt(var + GN_EPS) * lnw_ref[:, ch] + lnb_ref[:, ch]
        return (y + ops['bonus']) * ops['g'][:, ch]

    def lru():
        c0 = RWKV_SHIFT_COLS
        xc_b = []
        for rs in blocks:
            xc = cb_ref[...]
            for j in range(LRU_CONV):
                xc = xc + cw_ref[j:j + 1, :] * lagged(rs, slice(c0, c0 + LRU_WIDTH), LRU_CONV - 1 - j)
            xc_b.append(xc)
            yield
        xc = cat(xc_b)
        gate_a = _half_dot(xc, gaw_ref) + gab_ref[...]
        gate_x = _half_dot(xc, gxw_ref) + gxb_ref[...]
        yield
        neg_c_softplus = -LRU_C * jax.nn.softplus(-lam_ref[...])
        in_group = lax.broadcasted_iota(jnp.int32, (ROW_BLOCK, LRU_WIDTH), 0) % SUBLANES
        parts = []
        for b, rs in enumerate(blocks):
            log_a = neg_c_softplus * jax.nn.sigmoid(gate_a[rs])
            sa = jnp.exp(log_a)
            su = xc_b[b] * jax.nn.sigmoid(gate_x[rs]) * jnp.sqrt(1.0 - jnp.exp(2.0 * log_a))
            for d in (1, 2, 4):
                keep = in_group >= d
                su, sa = (su + sa * jnp.where(keep, pltpu.roll(su, d, 0), 0.0),
                          sa * jnp.where(keep, pltpu.roll(sa, d, 0), 1.0))
            gate_b = jax.nn.gelu(current(rs, slice(c0 + LRU_WIDTH, c0 + 2 * LRU_WIDTH)))
            parts.append((su, sa, gate_b))
            yield
        h_prev = hcar_s[t, 0:1, :]
        h_groups = []
        for su, sa, gate_b in parts:
            for gidx in range(ROW_BLOCK // SUBLANES):
                gs = slice(gidx * SUBLANES, (gidx + 1) * SUBLANES)
                hg = su[gs] + sa[gs] * h_prev
                h_groups.append(hg * gate_b[gs])
                h_prev = hg[SUBLANES - 1:SUBLANES, :]
            yield
        hcar_s[t] = jnp.broadcast_to(h_prev, hcar_s.shape[1:])
        return jnp.concatenate(h_groups, axis=0)

    def output(y_a, y_b):
        y_cat = jnp.concatenate([y_a, y_b], axis=1).astype(BF16)
        return x + jnp.dot(y_cat, wout_ref[...], preferred_element_type=F32)

    return dict(inproj=inproj, operands=lambda: operands(0), wkv=lambda ops: wkv(0, ops), lru=lru,
                finish=lambda y, ops: finish(0, y, ops), output=output)


def _even_body(x_ref, *refs, tm):
    o_ref, p_s, state_s, hcar_s = refs[-4:]
    i = pl.program_id(1)

    @pl.when(i == 0)
    def _():
        state_s[...] = jnp.zeros_like(state_s)
        hcar_s[...] = jnp.zeros_like(hcar_s)
        p_s[...] = jnp.zeros_like(p_s)

    t0, t1 = [_even_tile(t, x_ref, *refs[:-4], p_s, state_s, hcar_s, tm=tm) for t in range(2)]
    _run(t0['inproj']())
    ops0, _ = _run_together(t0['operands'](), t1['inproj']())
    y0, ops1 = _run_together(t0['wkv'](ops0), t1['operands']())
    y1, (yb0, ya0, yb1) = _run_together(
        t1['wkv'](ops1), _chain(t0['lru'](), t0['finish'](y0, ops0), t1['lru']()))
    o_ref[0] = t0['output'](ya0, yb0)
    ya1 = _run(t1['finish'](y1, ops1))
    o_ref[1] = t1['output'](ya1, yb1)


def _block_diag(w):
    h, n, _ = w.shape
    eye = jnp.eye(h, dtype=w.dtype)
    return (eye[:, None, :, None] * w[:, :, None, :]).reshape(h * n, h * n)


def _row(v):
    return v.reshape(1, -1).astype(F32)


def _const_spec(shape):
    nd = len(shape)
    return pl.BlockSpec(shape, lambda *_: (0,) * nd, pipeline_mode=pl.Buffered(1))


def _even_layer(x2, bsz, seq, norm_g, w_in, mu, w0, w2, a0, a2, g2, k_k, k_a, r_k, ln_w, ln_b,
                conv_w, conv_b, gate_a_w, gate_a_b, gate_x_w, gate_x_b, lru_lambda, w_out):
    tm = TM_MIX
    nt = seq // tm
    rw = RWKV_WIDTH
    wwa = jnp.zeros((W_LORA + A_LORA, 2 * rw), F32)
    wwa = wwa.at[:W_LORA, :rw].set(w2).at[W_LORA:, rw:].set(a2).astype(BF16)

    def halves(w):
        hh = w.shape[0] // 2
        return jnp.stack([_block_diag(w[:hh]), _block_diag(w[hh:])]).astype(BF16)

    hones = halves(jnp.ones((RWKV_HEADS, RWKV_HEAD, RWKV_HEAD), F32))
    consts = [
        _row(norm_g), w_in.astype(BF16), _row(mu), _row(w0), wwa, _row(a0), g2.astype(BF16),
        _row(k_k), _row(k_a), _row(r_k), _row(ln_w), _row(ln_b), conv_w.astype(F32), _row(conv_b),
        halves(gate_a_w), _row(gate_a_b), halves(gate_x_w),
        _row(gate_x_b), _row(lru_lambda), hones, w_out.astype(BF16),
    ]
    assert bsz % 2 == 0
    x3 = x2.reshape(2, (bsz // 2) * seq, D_MODEL)
    xspec = pl.BlockSpec((2, tm, D_MODEL), lambda b, i: (0, b * nt + i, 0))
    out = pl.pallas_call(
        functools.partial(_even_body, tm=tm),
        grid=(bsz // 2, nt),
        in_specs=[xspec] + [_const_spec(c.shape) for c in consts],
        out_specs=xspec,
        out_shape=jax.ShapeDtypeStruct(x3.shape, F32),
        scratch_shapes=[
            pltpu.VMEM((2, SUBLANES, EVEN_IN_COLS), F32),
            pltpu.VMEM((2, RWKV_HEADS, RWKV_HEAD, RWKV_HEAD), F32),
            pltpu.VMEM((2, SUBLANES, LRU_WIDTH), F32),
        ],
        compiler_params=pltpu.CompilerParams(
            dimension_semantics=("arbitrary", "arbitrary"), vmem_limit_bytes=VMEM_LIMIT),
        name="even_mixer",
    )(x3, *consts)
    return out.reshape(x2.shape)


def _s5_param_body(are_ref, aim_ref, ldt_ref, bre_ref, bim_ref, abre_ref, abim_ref, bbre_ref, bbim_ref):
    lam_re = jnp.minimum(are_ref[...], -1e-4)
    lam_im = aim_ref[...]
    dt = jnp.exp(ldt_ref[...])
    mag = jnp.exp(lam_re * dt)
    ab_re = mag * jnp.cos(lam_im * dt)
    ab_im = mag * jnp.sin(lam_im * dt)
    den = lam_re * lam_re + lam_im * lam_im
    zr = ab_re - 1.0
    q_re = (zr * lam_re + ab_im * lam_im) / den
    q_im = (ab_im * lam_re - zr * lam_im) / den
    abre_ref[...] = ab_re
    abim_ref[...] = ab_im
    for c in range(S5_GROUP):
        bbre_ref[c] = q_re * bre_ref[c] - q_im * bim_ref[c]
        bbim_ref[c] = q_re * bim_ref[c] + q_im * bre_ref[c]


def _s5_params(a_re, a_im, log_dt, b_re, b_im):
    g, n = S5_GROUPS, S5_STATE
    shp = jax.ShapeDtypeStruct
    return pl.pallas_call(
        _s5_param_body,
        out_shape=(shp((g, n), F32), shp((g, n), F32),
                   shp((S5_GROUP, g, n), F32), shp((S5_GROUP, g, n), F32)),
        name="s5_params",
    )(a_re.astype(F32), a_im.astype(F32), log_dt.reshape(g, 1).astype(F32),
      jnp.transpose(b_re, (2, 0, 1)).astype(F32), jnp.transpose(b_im, (2, 0, 1)).astype(F32))


def _odd_body(x_ref, perm_ref, permt_ref, ng_ref, win_ref, bre_ref, bim_ref, cre_ref, cim_ref,
              are_ref, aim_ref, d_ref, wglu_ref, o_ref, sre_s, sim_s, car_s, *, tt):
    i = pl.program_id(1)
    rows = SUBLANES * tt
    nstrip = sre_s.shape[1]
    nblk = bre_ref.shape[0]
    per = nstrip // nblk
    cw = bre_ref.shape[1]

    @pl.when(i == 0)
    def _():
        car_s[...] = jnp.zeros_like(car_s)

    def prepare(sub):
        x = x_ref[:, sub].reshape(rows, D_MODEL)
        xn = _rms(x, ng_ref[...]).astype(BF16)
        xn = jnp.dot(perm_ref[...], xn, preferred_element_type=F32).astype(BF16)
        u = jnp.dot(xn, win_ref[...], preferred_element_type=F32)
        ub = u.astype(BF16)
        yield
        for j in range(nblk):
            uj = ub[:, j * cw:(j + 1) * cw]
            pr = jnp.dot(uj, bre_ref[j], preferred_element_type=F32)
            pi = jnp.dot(uj, bim_ref[j], preferred_element_type=F32)
            for q in range(per):
                sre_s[sub, j * per + q] = pr[:, q * S5_LANES:(q + 1) * S5_LANES]
                sim_s[sub, j * per + q] = pi[:, q * S5_LANES:(q + 1) * S5_LANES]
            yield
        return x, u

    def scan(sub):
        for s in range(nstrip):
            ar = are_ref[s]
            ai = aim_ref[s]
            xr = car_s[0, s]
            xi = car_s[1, s]
            for t in range(tt):
                rs = slice(t * SUBLANES, (t + 1) * SUBLANES)
                xr, xi = (ar * xr - ai * xi + sre_s[sub, s, rs, :],
                          ar * xi + ai * xr + sim_s[sub, s, rs, :])
                sre_s[sub, s, rs, :] = xr
                sim_s[sub, s, rs, :] = xi
            car_s[0, s] = xr
            car_s[1, s] = xi
            yield

    def finish(sub, x, u):
        ys = []
        for j in range(nblk):
            acc = None
            for q in range(per):
                rsl = slice(q * S5_LANES, (q + 1) * S5_LANES)
                t = (_dot(sre_s[sub, j * per + q], cre_ref[j, rsl, :])
                     - _dot(sim_s[sub, j * per + q], cim_ref[j, rsl, :]))
                acc = t if acc is None else acc + t
            ys.append(acc)
            yield
        y = jnp.concatenate(ys, axis=1)
        yy = jax.nn.gelu(y + d_ref[...] * u).astype(BF16)
        yy = jnp.dot(permt_ref[...], yy, preferred_element_type=F32).astype(BF16)
        yield
        og = jnp.dot(yy, wglu_ref[...], preferred_element_type=F32)
        out = x + og[:, 0:D_MODEL] * jax.nn.sigmoid(og[:, D_MODEL:2 * D_MODEL])
        o_ref[:, sub] = out.reshape(SUBLANES, tt, D_MODEL)

    xu = [_run(prepare(0))]
    for sub in range(ODD_SUBTILES):
        side = []
        if sub > 0:
            side.append(finish(sub - 1, *xu[sub - 1]))
        if sub + 1 < ODD_SUBTILES:
            side.append(prepare(sub + 1))
        _, results = _run_together(scan(sub), _chain(*side))
        if sub + 1 < ODD_SUBTILES:
            xu.append(results[-1])
    _run(finish(ODD_SUBTILES - 1, *xu[-1]))


def _odd_layer(x2, bsz, seq, norm_g, w_in, a_re, a_im, log_dt, b_re, b_im, c_re, c_im, d_skip, w_glu):
    tt = TT_ODD
    assert bsz % SUBLANES == 0 and seq % tt == 0
    nt = seq // tt
    rows = SUBLANES * tt
    g, n, gc = S5_GROUPS, S5_STATE, S5_GROUP
    gpb = 8
    nblk = g // gpb
    nstrip = g * n // S5_LANES
    ab_re, ab_im, bb_re, bb_im = _s5_params(a_re, a_im, log_dt, b_re, b_im)
    eye = jnp.eye(gpb, dtype=F32)

    def b_blocks(bb):
        t = bb.reshape(gc, nblk, gpb, n)
        t = jnp.einsum('cjgn,gh->jgchn', t, eye)
        return t.reshape(nblk, gpb * gc, gpb * n).astype(BF16)

    def c_blocks(cc):
        t = cc.astype(F32).reshape(nblk, gpb, gc, n)
        t = jnp.einsum('jgcn,gh->jgnhc', t, eye)
        return t.reshape(nblk, gpb * n, gpb * gc).astype(BF16)

    def strips(p):
        return jnp.broadcast_to(p.reshape(nstrip, 1, S5_LANES), (nstrip, SUBLANES, S5_LANES))

    dst = jnp.arange(rows)
    src = (dst % SUBLANES) * tt + dst // SUBLANES
    perm = (src[:, None] == jnp.arange(rows)[None, :]).astype(BF16)

    consts = [perm, perm.T, _row(norm_g), w_in.astype(BF16), b_blocks(bb_re), b_blocks(bb_im),
              c_blocks(c_re), c_blocks(c_im), strips(ab_re), strips(ab_im), _row(d_skip),
              w_glu.astype(BF16)]
    assert nt % ODD_SUBTILES == 0
    x4 = x2.reshape(bsz, nt, tt, D_MODEL)
    xspec = pl.BlockSpec((SUBLANES, ODD_SUBTILES, tt, D_MODEL), lambda b, i: (b, i, 0, 0))
    out = pl.pallas_call(
        functools.partial(_odd_body, tt=tt),
        grid=(bsz // SUBLANES, nt // ODD_SUBTILES),
        in_specs=[xspec] + [_const_spec(c.shape) for c in consts],
        out_specs=xspec,
        out_shape=jax.ShapeDtypeStruct(x4.shape, F32),
        scratch_shapes=[
            pltpu.VMEM((ODD_SUBTILES, nstrip, rows, S5_LANES), F32),
            pltpu.VMEM((ODD_SUBTILES, nstrip, rows, S5_LANES), F32),
            pltpu.VMEM((2, nstrip, SUBLANES, S5_LANES), F32),
        ],
        compiler_params=pltpu.CompilerParams(
            dimension_semantics=("arbitrary", "arbitrary"), vmem_limit_bytes=VMEM_LIMIT),
        name="odd_mixer",
    )(x4, *consts)
    return out.reshape(x2.shape)


def _ffn_body(x_ref, ng_ref, wup_ref, cw_ref, cb_ref, wd_ref, fg_ref, o_ref,
              h_s, act_s, *, tm, tiles_per_seq, final_norm):
    i = pl.program_id(0)
    tf = TF_FFN
    x = x_ref[...]
    xn = _rms(x, ng_ref[...]).astype(BF16)

    @pl.when(i % tiles_per_seq == 0)
    def _():
        h_s[...] = jnp.zeros_like(h_s)

    def conv(c0):
        h = jnp.dot(xn, wup_ref[:, c0:c0 + tf], preferred_element_type=F32)
        full = jnp.concatenate([h_s[:, c0:c0 + tf], h], axis=0)
        h_s[:, c0:c0 + tf] = h[tm - SUBLANES:tm, :]
        out = cb_ref[:, c0:c0 + tf] + cw_ref[FFN_CONV - 1:FFN_CONV, c0:c0 + tf] * h
        for t in range(FFN_CONV - 1):
            lag = FFN_CONV - 1 - t
            out = out + cw_ref[t:t + 1, c0:c0 + tf] * full[SUBLANES - lag:SUBLANES - lag + tm, :]
        return out

    for c in range(D_FF // tf):
        act = jax.nn.silu(conv(c * tf)) * conv(D_FF + c * tf)
        act_s[:, c * tf:(c + 1) * tf] = act.astype(BF16)
    y = x + jnp.dot(act_s[...], wd_ref[...], preferred_element_type=F32)
    if final_norm:
        y = _rms(y, fg_ref[...])
    o_ref[...] = y


def _layer_spec(stacked, layer):
    nd = stacked.ndim - 1
    return pl.BlockSpec((None,) + stacked.shape[1:], lambda *_: (layer,) + (0,) * nd,
                        pipeline_mode=pl.Buffered(1))


def _ffn_layer(x2, seq, layer, norm_g, w_up_all, conv_w, conv_b, w_down_all, final_g, final_norm):
    tm = TM_FFN
    m = x2.shape[0]
    consts = [_row(norm_g), w_up_all, conv_w.astype(F32), _row(conv_b), w_down_all, _row(final_g)]
    in_specs = [pl.BlockSpec((tm, D_MODEL), lambda i: (i, 0))]
    in_specs += [_layer_spec(c, layer) if c.ndim == 3 else _const_spec(c.shape) for c in consts]
    return pl.pallas_call(
        functools.partial(_ffn_body, tm=tm, tiles_per_seq=seq // tm, final_norm=final_norm),
        grid=(m // tm,),
        in_specs=in_specs,
        out_specs=pl.BlockSpec((tm, D_MODEL), lambda i: (i, 0)),
        out_shape=jax.ShapeDtypeStruct(x2.shape, F32),
        scratch_shapes=[
            pltpu.VMEM((SUBLANES, 2 * D_FF), F32),
            pltpu.VMEM((tm, D_FF), BF16),
        ],
        compiler_params=pltpu.CompilerParams(
            dimension_semantics=("arbitrary",), vmem_limit_bytes=VMEM_LIMIT),
        name="conv_ffn",
    )(x2, *consts)


def kernel(x, e_norm_g, e_w_in, e_mu, e_w0, e_w2, e_a0, e_a2, e_g2, e_k_k, e_k_a, e_r_k,
           e_ln_w, e_ln_b, e_conv_w, e_conv_b, e_gate_a_w, e_gate_a_b, e_gate_x_w, e_gate_x_b,
           e_lru_lambda, e_w_out, o_norm_g, o_w_in, o_A_re, o_A_im, o_log_dt, o_B_re, o_B_im,
           o_C_re, o_C_im, o_D, o_w_glu, f_norm_g, f_w_up, f_conv_w, f_conv_b, f_w_down,
           final_norm_g):
    bsz, seq, dm = x.shape
    assert dm == D_MODEL and seq % TM_FFN == 0 and seq % TM_MIX == 0
    depth = f_norm_g.shape[0]
    h = x.reshape(bsz * seq, dm).astype(F32)
    f_w_up_bf = f_w_up.astype(BF16)
    f_w_down_bf = f_w_down.astype(BF16)
    for i in range(depth):
        j = i // 2
        if i % 2 == 0:
            h = _even_layer(h, bsz, seq, e_norm_g[j], e_w_in[j], e_mu[j], e_w0[j], e_w2[j],
                            e_a0[j], e_a2[j], e_g2[j], e_k_k[j], e_k_a[j], e_r_k[j], e_ln_w[j],
                            e_ln_b[j], e_conv_w[j], e_conv_b[j], e_gate_a_w[j], e_gate_a_b[j],
                            e_gate_x_w[j], e_gate_x_b[j], e_lru_lambda[j], e_w_out[j])
        else:
            h = _odd_layer(h, bsz, seq, o_norm_g[j], o_w_in[j], o_A_re[j], o_A_im[j],
                           o_log_dt[j], o_B_re[j], o_B_im[j], o_C_re[j], o_C_im[j], o_D[j],
                           o_w_glu[j])
        h = _ffn_layer(h, seq, i, f_norm_g[i], f_w_up_bf, f_conv_w[i], f_conv_b[i], f_w_down_bf,
                       final_norm_g, final_norm=(i == depth - 1))
    return h.reshape(bsz, seq, dm).astype(x.dtype)
```
